```python
import jax, jax.numpy as jnp
from jax import lax
import numpy as np

D_MODEL = 4096
BATCH = 32
SEQ = 256
DEPTH = 2
DEC_BATCH = 8
DEC_SEQ = 2048
PAST_LEN = 512

GRID_W = 64
N_EVEN = (DEPTH + 1) // 2
N_ODD = DEPTH // 2
FOURIER_WIDTH = D_MODEL // 2
N_FOURIER_GROUPS = 4
FOURIER_GROUP = FOURIER_WIDTH // N_FOURIER_GROUPS
POOL_WIDTH = D_MODEL - FOURIER_WIDTH
POOL_WINDOWS = (2, 4, 8, 16)
POOL_GROUP = POOL_WIDTH // len(POOL_WINDOWS)
HEAD_DIM = 128
N_HEADS = D_MODEL // HEAD_DIM
NA_ROWS_MAX = 8
NA_COLS = 16
KEY_COLS = 2 * NA_COLS
N_COL_BLOCKS = GRID_W // NA_COLS
CTX_BLOCK = 128
D_FF = 11008
CONV_WIDTH = 3
ALPHA = float((2 * DEPTH) ** 0.25)
BETA = float((8 * DEPTH) ** -0.25)
LN_EPS = 1e-5
NEG = -1e30

kernel_name = "hybrid_fourier_pool_natten_diffusion_step"


def ln_plain(x):
    xf = x.astype(jnp.float32)
    mu = jnp.mean(xf, axis=-1, keepdims=True)
    var = jnp.mean(jnp.square(xf - mu), axis=-1, keepdims=True)
    return ((xf - mu) * lax.rsqrt(var + LN_EPS)).astype(x.dtype)


def ln_affine(x, g, b):
    xf = x.astype(jnp.float32)
    mu = jnp.mean(xf, axis=-1, keepdims=True)
    var = jnp.mean(jnp.square(xf - mu), axis=-1, keepdims=True)
    return ((xf - mu) * lax.rsqrt(var + LN_EPS) * g + b).astype(x.dtype)


def fourier_mix(u, w_four):
    B, L, _ = u.shape
    ug = u.reshape(B, L, N_FOURIER_GROUPS, FOURIER_GROUP).astype(jnp.float32)
    f = jnp.fft.fft2(ug, axes=(1, 3), norm="ortho").real.astype(u.dtype)
    y = jnp.einsum("blgc,gcd->blgd", f, w_four)
    return y.reshape(B, L, FOURIER_WIDTH)


def centred_mean_minus_self(u, w):
    L = u.shape[1]
    uf = u.astype(jnp.float32)
    cs = jnp.concatenate([jnp.zeros_like(uf[:, :1]), jnp.cumsum(uf, axis=1)], axis=1)
    t = np.arange(L)
    lo = np.clip(t - w // 2, 0, L)
    hi = np.clip(t - w // 2 + w, 0, L)
    cnt = jnp.asarray((hi - lo).astype(np.float32))[None, :, None]
    return ((cs[:, hi] - cs[:, lo]) / cnt - uf).astype(u.dtype)


def pool_mix(u, w_pool, pool_scale):
    B, L, _ = u.shape
    ug = u.reshape(B, L, len(POOL_WINDOWS), POOL_GROUP)
    pooled = jnp.stack([centred_mean_minus_self(ug[:, :, g], w) for g, w in enumerate(POOL_WINDOWS)], axis=2)
    y = jnp.einsum("blgc,gcd->blgd", pooled, w_pool).reshape(B, L, POOL_WIDTH)
    return y * pool_scale


def fourier_pool_mixer(h, w_in, w_four, w_pool, pool_scale, w_out):
    u = h @ w_in
    ya = fourier_mix(u[..., :FOURIER_WIDTH], w_four)
    yb = pool_mix(u[..., FOURIER_WIDTH:], w_pool, pool_scale)
    return jnp.concatenate([ya, yb], axis=-1) @ w_out


def qkv_heads(h, w_qkv):
    B, L, _ = h.shape
    qkv = (h @ w_qkv).reshape(B, L, 3, N_HEADS, HEAD_DIM)
    return qkv[:, :, 0], qkv[:, :, 1], qkv[:, :, 2]


def context_attention(q, k, v):
    B, H, Lc, Dh = q.shape
    nb = Lc // CTX_BLOCK
    qb = (q * (Dh ** -0.5)).reshape(B, H, nb, CTX_BLOCK, Dh)

    def block(i):
        qi = lax.dynamic_index_in_dim(qb, i, axis=2, keepdims=False)
        s = jnp.einsum("bhqd,bhkd->bhqk", qi, k).astype(jnp.float32)
        p = jax.nn.softmax(s, axis=-1).astype(v.dtype)
        return jnp.einsum("bhqk,bhkd->bhqd", p, v)

    o = lax.map(block, jnp.arange(nb))
    return jnp.transpose(o, (1, 0, 3, 2, 4)).reshape(B, Lc, H * Dh)


def neighbourhood_attention(q, k, v, k_ctx, v_ctx, rpb):
    B, L, H, Dh = q.shape
    rows = L // GRID_W
    kr = min(NA_ROWS_MAX, rows)
    qg = (q * (Dh ** -0.5)).reshape(B, rows, N_COL_BLOCKS, NA_COLS, H, Dh)
    kg = k.reshape(B, rows, GRID_W, H, Dh)
    vg = v.reshape(B, rows, GRID_W, H, Dh)
    qcol = np.arange(GRID_W).reshape(N_COL_BLOCKS, NA_COLS)
    col_start = np.clip(qcol - NA_COLS // 2, 0, GRID_W - NA_COLS)
    kcol0 = np.clip(np.arange(N_COL_BLOCKS) * NA_COLS - NA_COLS // 2, 0, GRID_W - KEY_COLS)
    kcol = kcol0[:, None] + np.arange(KEY_COLS)
    col_mask = jnp.asarray((kcol[:, None, :] >= col_start[:, :, None]) &
                           (kcol[:, None, :] < col_start[:, :, None] + NA_COLS))
    dc_idx = np.clip(kcol[:, None, :] - qcol[:, :, None] + NA_COLS - 1, 0, 2 * NA_COLS - 2)
    rpb_col = rpb[:, :, dc_idx]

    def row_step(r):
        rs = jnp.clip(r - kr // 2, 0, rows - kr)
        kb = lax.dynamic_slice_in_dim(kg, rs, kr, axis=1)[:, :, kcol]
        vb = lax.dynamic_slice_in_dim(vg, rs, kr, axis=1)[:, :, kcol]
        qr = lax.dynamic_index_in_dim(qg, r, axis=1, keepdims=False)
        s_loc = jnp.einsum("bjqhd,bkjchd->bhjqkc", qr, kb).astype(jnp.float32)
        dr_idx = rs + jnp.arange(kr) - r + NA_ROWS_MAX - 1
        bias = jnp.transpose(rpb_col[:, dr_idx], (0, 2, 3, 1, 4))
        s_loc = jnp.where(col_mask[None, None, :, :, None, :], s_loc + bias[None], NEG)
        s_ctx = jnp.einsum("bjqhd,bhnd->bhjqn", qr, k_ctx).astype(jnp.float32)
        s = jnp.concatenate([s_loc.reshape(B, H, N_COL_BLOCKS, NA_COLS, kr * KEY_COLS), s_ctx], axis=-1)
        p = jax.nn.softmax(s, axis=-1).astype(v.dtype)
        p_loc = p[..., :kr * KEY_COLS].reshape(B, H, N_COL_BLOCKS, NA_COLS, kr, KEY_COLS)
        p_ctx = p[..., kr * KEY_COLS:]
        return (jnp.einsum("bhjqkc,bkjchd->bjqhd", p_loc, vb) +
                jnp.einsum("bhjqn,bhnd->bjqhd", p_ctx, v_ctx))

    o = lax.map(row_step, jnp.arange(rows))
    return jnp.moveaxis(o, 0, 1).reshape(B, L, H * Dh)


def conv_ffn(h, w_up, conv_w, conv_b, w_down):
    u = h @ w_up
    L = u.shape[1]
    pad = CONV_WIDTH // 2
    up = jnp.pad(u, ((0, 0), (pad, CONV_WIDTH - 1 - pad), (0, 0)))
    u = sum(up[:, j:j + L] * conv_w[j] for j in range(CONV_WIDTH)) + conv_b
    a, g = jnp.split(u, 2, axis=-1)
    return (jax.nn.silu(g) * a) @ w_down


def trunk(x, cond, ctx_k, ctx_v, w_ada, b_ada, ln1_g, ln1_b, ln2_g, ln2_b,
          w_in, w_four, w_pool, pool_scale, w_out_a, w_qkv, rpb, w_out_c,
          w_up, conv_w, conv_b, w_down):
    is_context = ctx_k is None
    new_k, new_v = [], []
    for i in range(DEPTH):
        mod = (jax.nn.silu(cond) @ w_ada[i] + b_ada[i])[:, None, :]
        sh1, sc1, g1, sh2, sc2, g2 = jnp.split(mod, 6, axis=-1)
        h = ln_plain(x) * (1 + sc1) + sh1
        j = i // 2
        if i % 2 == 0:
            y = fourier_pool_mixer(h, w_in[j], w_four[j], w_pool[j], pool_scale[j], w_out_a[j])
        else:
            q, k, v = qkv_heads(h, w_qkv[j])
            if is_context:
                kt = jnp.transpose(k, (0, 2, 1, 3))
                vt = jnp.transpose(v, (0, 2, 1, 3))
                o = context_attention(jnp.transpose(q, (0, 2, 1, 3)), kt, vt)
                new_k.append(kt)
                new_v.append(vt)
            else:
                o = neighbourhood_attention(q, k, v, ctx_k[:, j], ctx_v[:, j], rpb[j])
            y = o @ w_out_c[j]
        x = ln_affine(ALPHA * x + g1 * y, ln1_g[i], ln1_b[i])
        h = ln_plain(x) * (1 + sc2) + sh2
        f = conv_ffn(h, w_up[i], conv_w[i], conv_b[i], w_down[i])
        x = ln_affine(ALPHA * x + g2 * f, ln2_g[i], ln2_b[i])
    return x, new_k, new_v


def setup_inputs(seed: int = 0) -> dict:
    key = jax.random.key(seed)
    ks = jax.random.split(key, 32)
    D = D_MODEL
    nrm = lambda k, shape, s: jax.random.normal(k, shape, jnp.float32) * s
    return {
        "x_prompt": nrm(ks[0], (BATCH, SEQ, D), 1.0),
        "x_sample": nrm(ks[1], (DEC_BATCH, DEC_SEQ, D), 1.0),
        "cache_k": nrm(ks[2], (DEC_BATCH, N_ODD, N_HEADS, PAST_LEN, HEAD_DIM), 1.0),
        "cache_v": nrm(ks[3], (DEC_BATCH, N_ODD, N_HEADS, PAST_LEN, HEAD_DIM), 1.0),
        "c": nrm(ks[4], (DEC_BATCH, D), 1.0),
        "c_ctx": nrm(ks[5], (D,), 1.0),
        "w_ada": nrm(ks[6], (DEPTH, D, 6 * D), 0.5 * D ** -0.5),
        "b_ada": nrm(ks[7], (DEPTH, 6 * D), 0.01),
        "ln1_g": 1.0 + nrm(ks[8], (DEPTH, D), 0.02),
        "ln1_b": nrm(ks[9], (DEPTH, D), 0.02),
        "ln2_g": 1.0 + nrm(ks[10], (DEPTH, D), 0.02),
        "ln2_b": nrm(ks[11], (DEPTH, D), 0.02),
        "w_in": nrm(ks[12], (N_EVEN, D, D), D ** -0.5),
        "w_four": nrm(ks[13], (N_EVEN, N_FOURIER_GROUPS, FOURIER_GROUP, FOURIER_GROUP), FOURIER_GROUP ** -0.5),
        "w_pool": nrm(ks[14], (N_EVEN, len(POOL_WINDOWS), POOL_GROUP, POOL_GROUP), POOL_GROUP ** -0.5),
        "pool_scale": 1.0 + nrm(ks[15], (N_EVEN, POOL_WIDTH), 0.1),
        "w_out_a": nrm(ks[16], (N_EVEN, D, D), BETA * D ** -0.5),
        "w_qkv": nrm(ks[17], (N_ODD, D, 3 * D), D ** -0.5),
        "rpb": nrm(ks[18], (N_ODD, N_HEADS, 2 * NA_ROWS_MAX - 1, 2 * NA_COLS - 1), 0.1),
        "w_out_c": nrm(ks[19], (N_ODD, D, D), BETA * D ** -0.5),
        "w_up": nrm(ks[20], (DEPTH, D, 2 * D_FF), D ** -0.5),
        "conv_w": nrm(ks[21], (DEPTH, CONV_WIDTH, 2 * D_FF), CONV_WIDTH ** -0.5),
        "conv_b": nrm(ks[22], (DEPTH, 2 * D_FF), 0.01),
        "w_down": nrm(ks[23], (DEPTH, D_FF, D), BETA * D_FF ** -0.5),
    }


def reference(x_prompt, x_sample, cache_k, cache_v, c, c_ctx, w_ada, b_ada, ln1_g, ln1_b, ln2_g, ln2_b,
              w_in, w_four, w_pool, pool_scale, w_out_a, w_qkv, rpb, w_out_c,
              w_up, conv_w, conv_b, w_down):
    weights = (w_ada, b_ada, ln1_g, ln1_b, ln2_g, ln2_b, w_in, w_four, w_pool, pool_scale, w_out_a,
               w_qkv, rpb, w_out_c, w_up, conv_w, conv_b, w_down)
    y_prompt, ks_list, vs_list = trunk(x_prompt, c_ctx[None, :], None, None, *weights)
    new_k = jnp.stack(ks_list, axis=1)
    new_v = jnp.stack(vs_list, axis=1)
    y_sample, _, _ = trunk(x_sample, c, cache_k, cache_v, *weights)
    return (y_prompt, y_sample, new_k, new_v)
```

```python
import functools

import numpy as np
import jax
import jax.numpy as jnp
from jax import lax
from jax.experimental import pallas as pl
from jax.experimental.pallas import tpu as pltpu

F32 = jnp.float32
BF16 = jnp.bfloat16

LN_EPS = 1e-5
NEG = -1e30
GRID_W = 64
NA_ROWS = 8
NA_COLS = 16
HEAD_DIM = 128
POOL_WINDOWS = (2, 4, 8, 16)
COND_ROWS = 16
NA_TILE_ROWS = 4
NA_WIN_ROWS = 12
LN_CHUNK = 32
MIB = 1024 * 1024


def _cparams(semantics, vmem_mib):
    return pltpu.CompilerParams(dimension_semantics=semantics, vmem_limit_bytes=int(vmem_mib * MIB))


def _tile(dim, pref):
    t = min(dim, pref)
    assert dim % t == 0, (dim, pref)
    return t


def _ada_kernel(c_ref, w_ref, b_ref, o_ref):
    c = c_ref[...]
    s = (c * jax.nn.sigmoid(c)).astype(BF16)
    o_ref[...] = jnp.dot(s, w_ref[...].astype(BF16), preferred_element_type=F32) + b_ref[...]


def ada_modulation(cond, w_ada, b_ada):
    depth, d, n = w_ada.shape
    tn = _tile(n, 512)
    return pl.pallas_call(
        _ada_kernel,
        grid=(depth, n // tn),
        in_specs=[
            pl.BlockSpec((COND_ROWS, d), lambda l, j: (0, 0)),
            pl.BlockSpec((None, d, tn), lambda l, j: (l, 0, j)),
            pl.BlockSpec((None, 1, tn), lambda l, j: (l, 0, j)),
        ],
        out_specs=pl.BlockSpec((None, COND_ROWS, tn), lambda l, j: (l, 0, j)),
        out_shape=jax.ShapeDtypeStruct((depth, COND_ROWS, n), F32),
        compiler_params=_cparams(("arbitrary", "arbitrary"), 40),
        name="ada_modulation",
    )(cond, w_ada, b_ada.reshape(depth, 1, n))


def _normalise(x):
    mu = jnp.mean(x, axis=-1, keepdims=True)
    xc = x - mu
    var = jnp.mean(xc * xc, axis=-1, keepdims=True)
    return xc * lax.rsqrt(var + LN_EPS)


def _row_chunks(n_rows, body):
    chunk = min(LN_CHUNK, n_rows)
    assert n_rows % chunk == 0

    def step(c, carry):
        body(pl.ds(pl.multiple_of(c * chunk, chunk), chunk))
        return carry

    lax.fori_loop(0, n_rows // chunk, step, 0)


def _ln_mod_kernel(x_ref, sc_ref, sh_ref, h_ref):
    scale = 1.0 + sc_ref[...]
    shift = sh_ref[...]

    def body(rows):
        h_ref[rows, :] = (_normalise(x_ref[rows, :]) * scale + shift).astype(h_ref.dtype)

    _row_chunks(x_ref.shape[0], body)


def _mod_spec(d, row_of_tile, which):
    return pl.BlockSpec((None, 1, d), lambda i: (row_of_tile(i), 0, which))


def ln_modulate(x, mod, row_of, which_scale, which_shift):
    m, d = x.shape
    tm = _tile(m, 512)
    row = lambda i: row_of(i * tm)
    return pl.pallas_call(
        _ln_mod_kernel,
        grid=(m // tm,),
        in_specs=[pl.BlockSpec((tm, d), lambda i: (i, 0)),
                  _mod_spec(d, row, which_scale), _mod_spec(d, row, which_shift)],
        out_specs=pl.BlockSpec((tm, d), lambda i: (i, 0)),
        out_shape=jax.ShapeDtypeStruct((m, d), BF16),
        compiler_params=_cparams(("arbitrary",), 40),
        name="ln_modulate",
    )(x, mod, mod)


def _res_ln_kernel(*refs, alpha, with_next):
    if with_next:
        x_ref, y_ref, g_ref, gam_ref, bet_ref, sc_ref, sh_ref, xo_ref, h_ref = refs
        scale = 1.0 + sc_ref[...]
        shift = sh_ref[...]
    else:
        x_ref, y_ref, g_ref, gam_ref, bet_ref, xo_ref = refs
    gate, gamma, beta = g_ref[...], gam_ref[...], bet_ref[...]

    def body(rows):
        z = alpha * x_ref[rows, :] + gate * y_ref[rows, :].astype(F32)
        xn = _normalise(z) * gamma + beta
        xo_ref[rows, :] = xn
        if with_next:
            h_ref[rows, :] = (_normalise(xn) * scale + shift).astype(h_ref.dtype)

    _row_chunks(x_ref.shape[0], body)


def residual_ln(x, y, mod, row_of, which_gate, gamma, beta, alpha, nxt=None):
    m, d = x.shape
    tm = _tile(m, 256)
    row = lambda i: row_of(i * tm)
    tile = pl.BlockSpec((tm, d), lambda i: (i, 0))
    vec = pl.BlockSpec((1, d), lambda i: (0, 0))
    in_specs = [tile, tile, _mod_spec(d, row, which_gate), vec, vec]
    args = [x, y, mod, gamma.reshape(1, d), beta.reshape(1, d)]
    out_specs = [tile]
    out_shape = [jax.ShapeDtypeStruct((m, d), F32)]
    if nxt is not None:
        in_specs += [_mod_spec(d, row, nxt[1]), _mod_spec(d, row, nxt[2])]
        args += [nxt[0], nxt[0]]
        out_specs.append(tile)
        out_shape.append(jax.ShapeDtypeStruct((m, d), BF16))
    out = pl.pallas_call(
        functools.partial(_res_ln_kernel, alpha=alpha, with_next=nxt is not None),
        grid=(m // tm,),
        in_specs=in_specs,
        out_specs=out_specs,
        out_shape=out_shape,
        compiler_params=_cparams(("arbitrary",), 48),
        name="residual_ln",
    )(*args)
    return (out[0], out[1]) if nxt is not None else (out[0], None)


def _mm_kernel(*refs, n_parts, scale, head_major):
    a_refs, w_refs, o_ref = refs[:n_parts], refs[n_parts:2 * n_parts], refs[2 * n_parts]
    acc = None
    for a_ref, w_ref in zip(a_refs, w_refs):
        part = jnp.dot(a_ref[...], w_ref[...], preferred_element_type=F32)
        acc = part if acc is None else acc + part
    if scale is not None:
        acc = acc * scale
    if not head_major:
        o_ref[...] = acc.astype(o_ref.dtype)
        return
    bt, nh, lb, dh = o_ref.shape
    for b in range(bt):
        for h in range(nh):
            o_ref[b, h] = acc[b * lb:(b + 1) * lb, h * dh:(h + 1) * dh].astype(o_ref.dtype)


def matmul(a_parts, w, *, n, col_off=0, out_dtype=BF16, scale=None, seq_len=None, tm=1024, tn=1024):
    m, kp = a_parts[0].shape
    assert all(a.shape == (m, kp) for a in a_parts) and w.shape[0] == kp * len(a_parts)
    tm, tn = _tile(m, tm), _tile(n, tn)
    assert col_off % tn == 0
    joff = col_off // tn
    in_specs = [pl.BlockSpec((tm, kp), lambda i, j: (i, 0)) for _ in a_parts]
    in_specs += [pl.BlockSpec((kp, tn), functools.partial(lambda i, j, p: (p, j + joff), p=p))
                 for p in range(len(a_parts))]
    if seq_len is None:
        out_spec = pl.BlockSpec((tm, tn), lambda i, j: (i, j))
        out_shape = jax.ShapeDtypeStruct((m, n), out_dtype)
    else:
        nh = tn // HEAD_DIM
        out_shape = jax.ShapeDtypeStruct((m // seq_len, n // HEAD_DIM, seq_len, HEAD_DIM), out_dtype)
        if tm >= seq_len:
            out_spec = pl.BlockSpec((tm // seq_len, nh, seq_len, HEAD_DIM), lambda i, j: (i, j, 0, 0))
        else:
            per = seq_len // tm
            out_spec = pl.BlockSpec((1, nh, tm, HEAD_DIM), lambda i, j: (i // per, j, i % per, 0))
    return pl.pallas_call(
        functools.partial(_mm_kernel, n_parts=len(a_parts), scale=scale, head_major=seq_len is not None),
        grid=(m // tm, n // tn),
        in_specs=in_specs,
        out_specs=out_spec,
        out_shape=out_shape,
        compiler_params=_cparams(("arbitrary", "arbitrary"), 56),
        name="matmul",
    )(*a_parts, *([w] * len(a_parts)))


def _mm_ksplit_kernel(a_ref, w_ref, o_ref):
    part = jnp.dot(a_ref[...], w_ref[...], preferred_element_type=F32)

    @pl.when(pl.program_id(2) == 0)
    def _():
        o_ref[...] = part

    @pl.when(pl.program_id(2) > 0)
    def _():
        o_ref[...] += part


def matmul_ksplit(a, w, *, tm=1024, tn=512, k_steps=2):
    m, k = a.shape
    n = w.shape[1]
    tm, tn = _tile(m, tm), _tile(n, tn)
    tk = k // k_steps
    assert tk * k_steps == k and tk % 128 == 0
    return pl.pallas_call(
        _mm_ksplit_kernel,
        grid=(m // tm, n // tn, k_steps),
        in_specs=[pl.BlockSpec((tm, tk), lambda i, j, s: (i, s)),
                  pl.BlockSpec((tk, tn), lambda i, j, s: (s, j))],
        out_specs=pl.BlockSpec((tm, tn), lambda i, j, s: (i, j)),
        out_shape=jax.ShapeDtypeStruct((m, n), F32),
        compiler_params=_cparams(("arbitrary", "arbitrary", "arbitrary"), 56),
        name="matmul_ksplit",
    )(a, w)


def _ffn_up_kernel(a_ref, halo_ref, wa_ref, wg_ref, cwa_ref, cwg_ref, cba_ref, cbg_ref, o_ref, *, seq_len):
    a = a_ref[...]
    halo = halo_ref[...]
    tm, tn = o_ref.shape
    row = lax.broadcasted_iota(jnp.int32, (tm, tn), 0)
    pos = (pl.program_id(0) * tm + row) & (seq_len - 1)
    tile_first, tile_last = row == 0, row == tm - 1
    seq_first, seq_last = pos == 0, pos == seq_len - 1

    def conv(w_ref, cw_ref, cb_ref):
        w = w_ref[...]
        u = jnp.dot(a, w, preferred_element_type=F32)
        uh = jnp.dot(halo, w, preferred_element_type=F32)
        prev = jnp.where(tile_first, uh[0:1], jnp.where(seq_first, 0.0, pltpu.roll(u, 1, 0)))
        nxt = jnp.where(tile_last, uh[1:2], jnp.where(seq_last, 0.0, pltpu.roll(u, tm - 1, 0)))
        cw = cw_ref[...]
        return prev * cw[0:1] + u * cw[1:2] + nxt * cw[2:3] + cb_ref[...]

    act = conv(wa_ref, cwa_ref, cba_ref)
    gate = conv(wg_ref, cwg_ref, cbg_ref)
    o_ref[...] = (gate * jax.nn.sigmoid(gate) * act).astype(o_ref.dtype)


def ffn_up(h, w_up, conv_w, conv_b, seq_len, *, tm=1024, tn=256):
    m, d = h.shape
    d_ff = w_up.shape[1] // 2
    tm, tn = _tile(m, tm), _tile(d_ff, tn)
    assert (tm % seq_len == 0 or seq_len % tm == 0) and seq_len & (seq_len - 1) == 0
    nt, nj = m // tm, d_ff // tn
    tiles = h.reshape(nt, tm, d)
    zero = jnp.zeros((1, d), h.dtype)
    starts = np.arange(nt) * tm
    before = jnp.concatenate([zero, tiles[:-1, -1]], axis=0)
    before = jnp.where(jnp.asarray(starts % seq_len == 0)[:, None], 0, before)
    after = jnp.concatenate([tiles[1:, 0], zero], axis=0)
    after = jnp.where(jnp.asarray((starts + tm) % seq_len == 0)[:, None], 0, after)
    halo = jnp.concatenate([before[:, None], after[:, None], jnp.zeros((nt, 6, d), h.dtype)], axis=1)
    cb = conv_b.reshape(1, 2 * d_ff)
    return pl.pallas_call(
        functools.partial(_ffn_up_kernel, seq_len=seq_len),
        grid=(nt, nj),
        in_specs=[
            pl.BlockSpec((tm, d), lambda i, j: (i, 0)),
            pl.BlockSpec((8, d), lambda i, j: (i, 0)),
            pl.BlockSpec((d, tn), lambda i, j: (0, j)),
            pl.BlockSpec((d, tn), lambda i, j: (0, j + nj)),
            pl.BlockSpec((3, tn), lambda i, j: (0, j)),
            pl.BlockSpec((3, tn), lambda i, j: (0, j + nj)),
            pl.BlockSpec((1, tn), lambda i, j: (0, j)),
            pl.BlockSpec((1, tn), lambda i, j: (0, j + nj)),
        ],
        out_specs=pl.BlockSpec((tm, tn), lambda i, j: (i, j)),
        out_shape=jax.ShapeDtypeStruct((m, d_ff), BF16),
        compiler_params=_cparams(("arbitrary", "arbitrary"), 48),
        name="ffn_up",
    )(h, halo.reshape(nt * 8, d), w_up, w_up, conv_w, conv_w, cb, cb)


def _dft_matrices(n):
    k = jnp.arange(n, dtype=jnp.int32)
    phase = ((k[:, None] * k[None, :]) % n).astype(F32) * (2.0 * np.pi / n)
    scale = 1.0 / np.sqrt(n)
    return (jnp.cos(phase) * scale).astype(BF16), (jnp.sin(phase) * scale).astype(BF16)


def _fourier_kernel(u_ref, cl_ref, sl_ref, cc_ref, sc_ref, w_ref, o_ref):
    u = u_ref[...]
    zc = jnp.dot(u, cc_ref[...], preferred_element_type=F32).astype(BF16)
    zs = jnp.dot(u, sc_ref[...], preferred_element_type=F32).astype(BF16)
    f = (jnp.dot(cl_ref[...], zc, preferred_element_type=F32)
         - jnp.dot(sl_ref[...], zs, preferred_element_type=F32))
    o_ref[...] = jnp.dot(f.astype(BF16), w_ref[...], preferred_element_type=F32).astype(o_ref.dtype)


def fourier_mix(u, w_four, seq_len):
    m = u.shape[0]
    groups, gw, _ = w_four.shape
    cl, sl = _dft_matrices(seq_len)
    cc, sc = _dft_matrices(gw)
    const = lambda r, c: pl.BlockSpec((r, c), lambda b, g: (0, 0), pipeline_mode=pl.Buffered(1))
    return pl.pallas_call(
        _fourier_kernel,
        grid=(m // seq_len, groups),
        in_specs=[
            pl.BlockSpec((seq_len, gw), lambda b, g: (b, g)),
            const(seq_len, seq_len), const(seq_len, seq_len), const(gw, gw), const(gw, gw),
            pl.BlockSpec((None, gw, gw), lambda b, g: (g, 0, 0)),
        ],
        out_specs=pl.BlockSpec((seq_len, gw), lambda b, g: (b, g)),
        out_shape=jax.ShapeDtypeStruct((m, groups * gw), BF16),
        compiler_params=_cparams(("arbitrary", "arbitrary"), 56),
        name="fourier_mix",
    )(u, cl, sl, cc, sc, w_four)


def _pool_matrices(seq_len):
    t = jnp.arange(seq_len, dtype=jnp.int32)
    bands, inv = [], []
    for w in POOL_WINDOWS:
        lo = jnp.clip(t - w // 2, 0, seq_len)
        hi = jnp.clip(t - w // 2 + w, 0, seq_len)
        bands.append(((t[None, :] >= lo[:, None]) & (t[None, :] < hi[:, None])).astype(BF16))
        inv.append(1.0 / (hi - lo).astype(F32))
    return jnp.stack(bands), jnp.stack(inv)[:, :, None]


def _pool_kernel(u_ref, band_ref, inv_ref, w_ref, ps_ref, o_ref):
    u = u_ref[...]
    pooled = jnp.dot(band_ref[...], u, preferred_element_type=F32) * inv_ref[...] - u.astype(F32)
    y = jnp.dot(pooled.astype(BF16), w_ref[...], preferred_element_type=F32)
    o_ref[...] = (y * ps_ref[...]).astype(o_ref.dtype)


def pool_mix(u, w_pool, pool_scale, seq_len, group_off):
    m = u.shape[0]
    groups, gw, _ = w_pool.shape
    band, inv = _pool_matrices(seq_len)
    return pl.pallas_call(
        _pool_kernel,
        grid=(groups, m // seq_len),
        in_specs=[
            pl.BlockSpec((seq_len, gw), lambda g, b: (b, g + group_off)),
            pl.BlockSpec((None, seq_len, seq_len), lambda g, b: (g, 0, 0)),
            pl.BlockSpec((None, seq_len, 1), lambda g, b: (g, 0, 0)),
            pl.BlockSpec((None, gw, gw), lambda g, b: (g, 0, 0)),
            pl.BlockSpec((1, gw), lambda g, b: (0, g)),
        ],
        out_specs=pl.BlockSpec((seq_len, gw), lambda g, b: (b, g)),
        out_shape=jax.ShapeDtypeStruct((m, groups * gw), BF16),
        compiler_params=_cparams(("arbitrary", "arbitrary"), 48),
        name="pool_mix",
    )(u, band, inv, w_pool, pool_scale.reshape(1, groups * gw))


_NT_DIMS = (((1,), (1,)), ((), ()))


def _ctx_attn_kernel(q_ref, k_ref, v_ref, o_ref):
    n_heads, _, dh = q_ref.shape
    for h in range(n_heads):
        k = k_ref[h].astype(BF16)
        v = v_ref[h].astype(BF16)
        s = lax.dot_general(q_ref[h], k, _NT_DIMS, preferred_element_type=F32)
        p = jnp.exp(s - jnp.max(s, axis=-1, keepdims=True))
        denom = jnp.sum(p, axis=-1, keepdims=True)
        o = jnp.dot(p.astype(BF16), v, preferred_element_type=F32) / denom
        o_ref[:, h * dh:(h + 1) * dh] = o.astype(o_ref.dtype)


def context_attention(q, k, v):
    b, n_heads, seq_len, dh = q.shape
    blk = pl.BlockSpec((None, n_heads, seq_len, dh), lambda i: (i, 0, 0, 0))
    return pl.pallas_call(
        _ctx_attn_kernel,
        grid=(b,),
        in_specs=[blk, blk, blk],
        out_specs=pl.BlockSpec((seq_len, n_heads * dh), lambda i: (i, 0)),
        out_shape=jax.ShapeDtypeStruct((b * seq_len, n_heads * dh), BF16),
        compiler_params=_cparams(("arbitrary",), 48),
        name="context_attention",
    )(q, k, v)


def _na_structure(rows):
    kr = min(NA_ROWS, rows)
    assert rows % NA_TILE_ROWS == 0 and rows >= NA_WIN_ROWS
    n_tiles = rows // NA_TILE_ROWS
    starts = np.clip(np.arange(n_tiles) * NA_TILE_ROWS - kr // 2, 0, rows - NA_WIN_ROWS)
    patterns, pattern_of = [], []
    for t in range(n_tiles):
        r = t * NA_TILE_ROWS + np.arange(NA_TILE_ROWS)[:, None]
        ka = starts[t] + np.arange(NA_WIN_ROWS)[None, :]
        rs = np.clip(r - kr // 2, 0, rows - kr)
        assert (rs >= starts[t]).all() and (rs + kr <= starts[t] + NA_WIN_ROWS).all()
        slot = np.where((ka >= rs) & (ka < rs + kr), ka - r + NA_ROWS - 1, 2 * NA_ROWS - 1)
        for p, known in enumerate(patterns):
            if (known == slot).all():
                pattern_of.append(p)
                break
        else:
            pattern_of.append(len(patterns))
            patterns.append(slot)
    return tuple(int(s) for s in starts), tuple(pattern_of), np.stack(patterns)


def _na_bias(rpb, dr_slot):
    n_heads = rpb.shape[0]
    qc = np.arange(GRID_W)[:, None]
    kc = np.arange(GRID_W)[None, :]
    col_start = np.clip(qc - NA_COLS // 2, 0, GRID_W - NA_COLS)
    col_ok = (kc >= col_start) & (kc < col_start + NA_COLS)
    dc = np.clip(kc - qc + NA_COLS - 1, 0, 2 * NA_COLS - 2)
    onehot = (dc.reshape(1, -1) == np.arange(2 * NA_COLS - 1)[:, None]).astype(np.float32)
    toep = jnp.einsum("hds,sn->hdn", rpb, jnp.asarray(onehot), precision=lax.Precision.HIGHEST)
    toep = jnp.where(jnp.asarray(col_ok)[None, None], toep.reshape(n_heads, 2 * NA_ROWS - 1, GRID_W, GRID_W), NEG)
    toep = jnp.concatenate([toep, jnp.full((n_heads, 1, GRID_W, GRID_W), NEG, F32)], axis=1)
    n_pat = dr_slot.shape[0]
    bias = toep[:, jnp.asarray(dr_slot)]
    bias = jnp.transpose(bias, (0, 1, 2, 4, 3, 5))
    return bias.reshape(n_heads, n_pat, NA_TILE_ROWS * GRID_W, NA_WIN_ROWS * GRID_W)


def _na_kernel(q_ref, k_ref, v_ref, ck_ref, cv_ref, bias_ref, o_ref, *, starts, pattern_of):
    tq = NA_TILE_ROWS * GRID_W
    tk = NA_WIN_ROWS * GRID_W
    ck = ck_ref[...].astype(BF16)
    cv = cv_ref[...].astype(BF16)
    for t, (start, pat) in enumerate(zip(starts, pattern_of)):
        q = q_ref[t * tq:(t + 1) * tq, :]
        kw = k_ref[start * GRID_W:start * GRID_W + tk, :]
        vw = v_ref[start * GRID_W:start * GRID_W + tk, :]
        s_loc = lax.dot_general(q, kw, _NT_DIMS, preferred_element_type=F32) + bias_ref[pat]
        s_ctx = lax.dot_general(q, ck, _NT_DIMS, preferred_element_type=F32)
        mx = jnp.maximum(jnp.max(s_loc, axis=-1, keepdims=True), jnp.max(s_ctx, axis=-1, keepdims=True))
        p_loc = jnp.exp(s_loc - mx)
        p_ctx = jnp.exp(s_ctx - mx)
        denom = jnp.sum(p_loc, axis=-1, keepdims=True) + jnp.sum(p_ctx, axis=-1, keepdims=True)
        o = (jnp.dot(p_loc.astype(BF16), vw, preferred_element_type=F32)
             + jnp.dot(p_ctx.astype(BF16), cv, preferred_element_type=F32)) / denom
        o_ref[t * tq:(t + 1) * tq, :] = o.astype(o_ref.dtype)


def neighbourhood_attention(q, k, v, ctx_k, ctx_v, layer, rpb):
    b, n_heads, seq_len, dh = q.shape
    lc = ctx_k.shape[3]
    starts, pattern_of, dr_slot = _na_structure(seq_len // GRID_W)
    bias = _na_bias(rpb, dr_slot)
    _, n_pat, tq, tk = bias.shape
    qkv = pl.BlockSpec((None, None, seq_len, dh), lambda h, i: (i, h, 0, 0))
    ctx = pl.BlockSpec((None, None, None, lc, dh), lambda h, i: (i, layer, h, 0, 0))
    return pl.pallas_call(
        functools.partial(_na_kernel, starts=starts, pattern_of=pattern_of),
        grid=(n_heads, b),
        in_specs=[qkv, qkv, qkv, ctx, ctx,
                  pl.BlockSpec((None, n_pat, tq, tk), lambda h, i: (h, 0, 0, 0))],
        out_specs=pl.BlockSpec((seq_len, dh), lambda h, i: (i, h)),
        out_shape=jax.ShapeDtypeStruct((b * seq_len, n_heads * dh), BF16),
        compiler_params=_cparams(("arbitrary", "arbitrary"), 48),
        name="neighbourhood_attention",
    )(q, k, v, ctx_k, ctx_v, bias)


SHIFT1, SCALE1, GATE1, SHIFT2, SCALE2, GATE2 = range(6)


def _trunk(x, seq_len, mod, row_of, ctx_kv, wts, alpha):
    (ln1_g, ln1_b, ln2_g, ln2_b, w_in, w_four, w_pool, pool_scale, w_out_a, w_qkv, rpb, w_out_c,
     w_up, conv_w, conv_b, w_down) = wts
    depth = mod.shape[0]
    d = x.shape[1]
    new_k, new_v = [], []
    h = ln_modulate(x, mod[0], row_of, SCALE1, SHIFT1)
    for i in range(depth):
        j = i // 2
        if i % 2 == 0:
            u = matmul([h], w_in[j], n=d)
            ya = fourier_mix(u, w_four[j], seq_len)
            yb = pool_mix(u, w_pool[j], pool_scale[j], seq_len, w_four.shape[1])
            y = matmul([ya, yb], w_out_a[j], n=d, out_dtype=F32)
        else:
            q = matmul([h], w_qkv[j], n=d, col_off=0, scale=HEAD_DIM ** -0.5, seq_len=seq_len)
            if ctx_kv is None:
                k = matmul([h], w_qkv[j], n=d, col_off=d, out_dtype=F32, seq_len=seq_len)
                v = matmul([h], w_qkv[j], n=d, col_off=2 * d, out_dtype=F32, seq_len=seq_len)
                new_k.append(k)
                new_v.append(v)
                o = context_attention(q, k, v)
            else:
                k = matmul([h], w_qkv[j], n=d, col_off=d, seq_len=seq_len)
                v = matmul([h], w_qkv[j], n=d, col_off=2 * d, seq_len=seq_len)
                o = neighbourhood_attention(q, k, v, ctx_kv[0], ctx_kv[1], j, rpb[j])
            y = matmul([o], w_out_c[j], n=d, out_dtype=F32)
        x, h = residual_ln(x, y, mod[i], row_of, GATE1, ln1_g[i], ln1_b[i], alpha, nxt=(mod[i], SCALE2, SHIFT2))
        f = matmul_ksplit(ffn_up(h, w_up[i], conv_w[i], conv_b[i], seq_len), w_down[i])
        nxt = (mod[i + 1], SCALE1, SHIFT1) if i + 1 < depth else None
        x, h = residual_ln(x, f, mod[i], row_of, GATE2, ln2_g[i], ln2_b[i], alpha, nxt=nxt)
    return x, new_k, new_v


def kernel(x_prompt, x_sample, cache_k, cache_v, c, c_ctx, w_ada, b_ada, ln1_g, ln1_b, ln2_g, ln2_b,
           w_in, w_four, w_pool, pool_scale, w_out_a, w_qkv, rpb, w_out_c, w_up, conv_w, conv_b, w_down):
    batch, seq, d = x_prompt.shape
    dec_batch, dec_seq, _ = x_sample.shape
    depth = w_ada.shape[0]
    alpha = float((2 * depth) ** 0.25)
    assert dec_batch < COND_ROWS

    cond = jnp.concatenate([c, c_ctx[None, :], jnp.zeros((COND_ROWS - dec_batch - 1, d), F32)], axis=0)
    mod = ada_modulation(cond, w_ada, b_ada).reshape(depth, COND_ROWS, 1, 6 * d)

    wts = (ln1_g, ln1_b, ln2_g, ln2_b,
           w_in.astype(BF16), w_four.astype(BF16), w_pool.astype(BF16), pool_scale, w_out_a.astype(BF16),
           w_qkv.astype(BF16), rpb, w_out_c.astype(BF16),
           w_up.astype(BF16), conv_w, conv_b, w_down.astype(BF16))

    y_prompt, ks, vs = _trunk(x_prompt.reshape(batch * seq, d), seq, mod, lambda r: dec_batch, None, wts, alpha)
    y_sample, _, _ = _trunk(x_sample.reshape(dec_batch * dec_seq, d), dec_seq, mod, lambda r: r // dec_seq,
                            (cache_k, cache_v), wts, alpha)
    return (y_prompt.reshape(batch, seq, d), y_sample.reshape(dec_batch, dec_seq, d),
            jnp.stack(ks, axis=1), jnp.stack(vs, axis=1))
```

```python
import functools

import numpy as np
import jax
import jax.numpy as jnp
from jax import lax
from jax.experimental import pallas as pl
from jax.experimental.pallas import tpu as pltpu

F32 = jnp.float32
BF16 = jnp.bfloat16

LN_EPS = 1e-5
NEG = -1e30
GRID_W = 64
NA_ROWS = 8
NA_COLS = 16
HEAD_DIM = 128
POOL_WINDOWS = (2, 4, 8, 16)
COND_ROWS = 16
NA_TILE_ROWS = 4
NA_WIN_ROWS = 12
LN_CHUNK = 32
MIB = 1024 * 1024


def _cparams(semantics, vmem_mib):
    return pltpu.CompilerParams(dimension_semantics=semantics, vmem_limit_bytes=int(vmem_mib * MIB))


def _tile(dim, pref):
    t = min(dim, pref)
    assert dim % t == 0, (dim, pref)
    return t


def _ada_kernel(c_ref, w_ref, b_ref, o_ref):
    c = c_ref[...]
    s = (c * jax.nn.sigmoid(c)).astype(BF16)
    o_ref[...] = jnp.dot(s, w_ref[...].astype(BF16), preferred_element_type=F32) + b_ref[...]


def ada_modulation(cond, w_ada, b_ada):
    depth, d, n = w_ada.shape
    tn = _tile(n, 512)
    return pl.pallas_call(
        _ada_kernel,
        grid=(depth, n // tn),
        in_specs=[
            pl.BlockSpec((COND_ROWS, d), lambda l, j: (0, 0)),
            pl.BlockSpec((None, d, tn), lambda l, j: (l, 0, j)),
            pl.BlockSpec((None, 1, tn), lambda l, j: (l, 0, j)),
        ],
        out_specs=pl.BlockSpec((None, COND_ROWS, tn), lambda l, j: (l, 0, j)),
        out_shape=jax.ShapeDtypeStruct((depth, COND_ROWS, n), F32),
        compiler_params=_cparams(("arbitrary", "arbitrary"), 40),
        name="ada_modulation",
    )(cond, w_ada, b_ada.reshape(depth, 1, n))


def _normalise(x):
    mu = jnp.mean(x, axis=-1, keepdims=True)
    xc = x - mu
    var = jnp.mean(xc * xc, axis=-1, keepdims=True)
    return xc * lax.rsqrt(var + LN_EPS)


def _row_chunks(n_rows, body):
    chunk = min(LN_CHUNK, n_rows)
    assert n_rows % chunk == 0

    def step(c, carry):
        body(pl.ds(pl.multiple_of(c * chunk, chunk), chunk))
        return carry

    lax.fori_loop(0, n_rows // chunk, step, 0)


def _ln_mod_kernel(x_ref, sc_ref, sh_ref, h_ref):
    scale = 1.0 + sc_ref[...]
    shift = sh_ref[...]

    def body(rows):
        h_ref[rows, :] = (_normalise(x_ref[rows, :]) * scale + shift).astype(h_ref.dtype)

    _row_chunks(x_ref.shape[0], body)


def _mod_spec(d, row_of_tile, which):
    return pl.BlockSpec((None, 1, d), lambda i: (row_of_tile(i), 0, which))


def ln_modulate(x, mod, row_of, which_scale, which_shift):
    m, d = x.shape
    tm = _tile(m, 512)
    row = lambda i: row_of(i * tm)
    return pl.pallas_call(
        _ln_mod_kernel,
        grid=(m // tm,),
        in_specs=[pl.BlockSpec((tm, d), lambda i: (i, 0)),
                  _mod_spec(d, row, which_scale), _mod_spec(d, row, which_shift)],
        out_specs=pl.BlockSpec((tm, d), lambda i: (i, 0)),
        out_shape=jax.ShapeDtypeStruct((m, d), BF16),
        compiler_params=_cparams(("arbitrary",), 40),
        name="ln_modulate",
    )(x, mod, mod)


def _res_ln_kernel(*refs, alpha, with_next):
    if with_next:
        x_ref, y_ref, g_ref, gam_ref, bet_ref, sc_ref, sh_ref, xo_ref, h_ref = refs
        scale = 1.0 + sc_ref[...]
        shift = sh_ref[...]
    else:
        x_ref, y_ref, g_ref, gam_ref, bet_ref, xo_ref = refs
    gate, gamma, beta = g_ref[...], gam_ref[...], bet_ref[...]

    def body(rows):
        z = alpha * x_ref[rows, :] + gate * y_ref[rows, :].astype(F32)
        xn = _normalise(z) * gamma + beta
        xo_ref[rows, :] = xn
        if with_next:
            h_ref[rows, :] = (_normalise(xn) * scale + shift).astype(h_ref.dtype)

    _row_chunks(x_ref.shape[0], body)


def residual_ln(x, y, mod, row_of, which_gate, gamma, beta, alpha, nxt=None):
    m, d = x.shape
    tm = _tile(m, 256)
    row = lambda i: row_of(i * tm)
    tile = pl.BlockSpec((tm, d), lambda i: (i, 0))
    vec = pl.BlockSpec((1, d), lambda i: (0, 0))
    in_specs = [tile, tile, _mod_spec(d, row, which_gate), vec, vec]
    args = [x, y, mod, gamma.reshape(1, d), beta.reshape(1, d)]
    out_specs = [tile]
    out_shape = [jax.ShapeDtypeStruct((m, d), F32)]
    if nxt is not None:
        in_specs += [_mod_spec(d, row, nxt[1]), _mod_spec(d, row, nxt[2])]
        args += [nxt[0], nxt[0]]
        out_specs.append(tile)
        out_shape.append(jax.ShapeDtypeStruct((m, d), BF16))
    out = pl.pallas_call(
        functools.partial(_res_ln_kernel, alpha=alpha, with_next=nxt is not None),
        grid=(m // tm,),
        in_specs=in_specs,
        out_specs=out_specs,
        out_shape=out_shape,
        compiler_params=_cparams(("arbitrary",), 48),
        name="residual_ln",
    )(*args)
    return (out[0], out[1]) if nxt is not None else (out[0], None)


def _mm_kernel(*refs, n_parts, scale, head_major):
    a_refs, w_refs, o_ref = refs[:n_parts], refs[n_parts:2 * n_parts], refs[2 * n_parts]
    acc = None
    for a_ref, w_ref in zip(a_refs, w_refs):
        part = jnp.dot(a_ref[...], w_ref[...], preferred_element_type=F32)
        acc = part if acc is None else acc + part
    if scale is not None:
        acc = acc * scale
    if not head_major:
        o_ref[...] = acc.astype(o_ref.dtype)
        return
    bt, nh, lb, dh = o_ref.shape
    for b in range(bt):
        for h in range(nh):
            o_ref[b, h] = acc[b * lb:(b + 1) * lb, h * dh:(h + 1) * dh].astype(o_ref.dtype)


def matmul(a_parts, w, layer, *, n, col_off=0, out_dtype=BF16, scale=None, seq_len=None, tm=1024, tn=1024):
    m, kp = a_parts[0].shape
    assert all(a.shape == (m, kp) for a in a_parts) and w.shape[1] == kp * len(a_parts)
    tm, tn = _tile(m, tm), _tile(n, tn)
    assert col_off % tn == 0
    joff = col_off // tn
    in_specs = [pl.BlockSpec((tm, kp), lambda i, j: (i, 0)) for _ in a_parts]
    in_specs += [pl.BlockSpec((None, kp, tn), functools.partial(lambda i, j, p: (layer, p, j + joff), p=p))
                 for p in range(len(a_parts))]
    if seq_len is None:
        out_spec = pl.BlockSpec((tm, tn), lambda i, j: (i, j))
        out_shape = jax.ShapeDtypeStruct((m, n), out_dtype)
    else:
        nh = tn // HEAD_DIM
        out_shape = jax.ShapeDtypeStruct((m // seq_len, n // HEAD_DIM, seq_len, HEAD_DIM), out_dtype)
        if tm >= seq_len:
            out_spec = pl.BlockSpec((tm // seq_len, nh, seq_len, HEAD_DIM), lambda i, j: (i, j, 0, 0))
        else:
            per = seq_len // tm
            out_spec = pl.BlockSpec((1, nh, tm, HEAD_DIM), lambda i, j: (i // per, j, i % per, 0))
    return pl.pallas_call(
        functools.partial(_mm_kernel, n_parts=len(a_parts), scale=scale, head_major=seq_len is not None),
        grid=(m // tm, n // tn),
        in_specs=in_specs,
        out_specs=out_spec,
        out_shape=out_shape,
        compiler_params=_cparams(("arbitrary", "arbitrary"), 56),
        name="matmul",
    )(*a_parts, *([w] * len(a_parts)))


def _mm_ksplit_kernel(a_ref, w_ref, o_ref, acc_ref, *, k_steps):
    part = jnp.dot(a_ref[...], w_ref[...], preferred_element_type=F32)
    s = pl.program_id(2)

    @pl.when(s == 0)
    def _():
        acc_ref[...] = part

    @pl.when((s > 0) & (s < k_steps - 1))
    def _():
        acc_ref[...] += part

    @pl.when(s == k_steps - 1)
    def _():
        o_ref[...] = (acc_ref[...] + part).astype(o_ref.dtype)


def matmul_ksplit(a, w, layer, *, out_dtype=BF16, tm=1024, tn=512, k_steps=2):
    m, k = a.shape
    n = w.shape[2]
    tm, tn = _tile(m, tm), _tile(n, tn)
    tk = k // k_steps
    assert k_steps >= 2 and tk * k_steps == k and tk % 128 == 0
    return pl.pallas_call(
        functools.partial(_mm_ksplit_kernel, k_steps=k_steps),
        grid=(m // tm, n // tn, k_steps),
        in_specs=[pl.BlockSpec((tm, tk), lambda i, j, s: (i, s)),
                  pl.BlockSpec((None, tk, tn), lambda i, j, s: (layer, s, j))],
        out_specs=pl.BlockSpec((tm, tn), lambda i, j, s: (i, j)),
        out_shape=jax.ShapeDtypeStruct((m, n), out_dtype),
        scratch_shapes=[pltpu.VMEM((tm, tn), F32)],
        compiler_params=_cparams(("arbitrary", "arbitrary", "arbitrary"), 56),
        name="matmul_ksplit",
    )(a, w)


def _ffn_up_kernel(a_ref, halo_ref, wa_ref, wg_ref, cwa_ref, cwg_ref, o_ref, *, seq_len):
    a = a_ref[...]
    halo = halo_ref[...]
    tm, tn = o_ref.shape
    sub = lax.broadcasted_iota(jnp.int32, (8, tn), 0)
    inner = range(seq_len, tm, seq_len)

    def patch(x, group, row, value):
        lo, hi = group * 8, (group + 1) * 8
        parts = [x[:lo]] * (lo > 0) + [jnp.where(sub == row, value, x[lo:hi])] + [x[hi:]] * (hi < tm)
        return jnp.concatenate(parts, axis=0)

    def conv(w_ref, cw_ref):
        w = w_ref[...]
        u = jnp.dot(a, w, preferred_element_type=F32)
        uh = jnp.dot(halo, w, preferred_element_type=F32)
        prev = patch(pltpu.roll(u, 1, 0), 0, 0, uh[7:8])
        nxt = patch(pltpu.roll(u, tm - 1, 0), tm // 8 - 1, 7, uh[0:1])
        for r in inner:
            prev = patch(prev, r // 8, 0, 0.0)
            nxt = patch(nxt, r // 8 - 1, 7, 0.0)
        cw = cw_ref[...]
        return prev * cw[0:1] + u * cw[1:2] + nxt * cw[2:3] + cw[3:4]

    act = conv(wa_ref, cwa_ref)
    half = 0.5 * conv(wg_ref, cwg_ref)
    o_ref[...] = ((half + half * jnp.tanh(half)) * act).astype(o_ref.dtype)


def ffn_up(h, w_up, conv_w, conv_b, layer, seq_len, *, tm=1024, tn=256):
    m, d = h.shape
    d_ff = w_up.shape[2] // 2
    tm, tn = _tile(m, tm), _tile(d_ff, tn)
    assert (tm % seq_len == 0 or seq_len % tm == 0) and tm % 16 == 0
    nt, nj = m // tm, d_ff // tn
    tiles = h.reshape(nt, tm, d)
    zero = jnp.zeros((1, d), h.dtype)
    starts = np.arange(nt) * tm
    before = jnp.concatenate([zero, tiles[:-1, -1]], axis=0)
    before = jnp.where(jnp.asarray(starts % seq_len == 0)[:, None], 0, before)
    after = jnp.concatenate([tiles[1:, 0], zero], axis=0)
    after = jnp.where(jnp.asarray((starts + tm) % seq_len == 0)[:, None], 0, after)
    halo = jnp.concatenate([after[:, None], jnp.zeros((nt, 6, d), h.dtype), before[:, None]], axis=1)
    taps = jnp.concatenate([conv_w[layer], conv_b[layer][None]], axis=0)
    return pl.pallas_call(
        functools.partial(_ffn_up_kernel, seq_len=seq_len),
        grid=(nt, nj),
        in_specs=[
            pl.BlockSpec((tm, d), lambda i, j: (i, 0)),
            pl.BlockSpec((8, d), lambda i, j: (i, 0)),
            pl.BlockSpec((None, d, tn), lambda i, j: (layer, 0, j)),
            pl.BlockSpec((None, d, tn), lambda i, j: (layer, 0, j + nj)),
            pl.BlockSpec((4, tn), lambda i, j: (0, j)),
            pl.BlockSpec((4, tn), lambda i, j: (0, j + nj)),
        ],
        out_specs=pl.BlockSpec((tm, tn), lambda i, j: (i, j)),
        out_shape=jax.ShapeDtypeStruct((m, d_ff), BF16),
        compiler_params=_cparams(("arbitrary", "arbitrary"), 48),
        name="ffn_up",
    )(h, halo.reshape(nt * 8, d), w_up, w_up, taps, taps)


def _dft_matrices(n):
    k = jnp.arange(n, dtype=jnp.int32)
    phase = ((k[:, None] * k[None, :]) % n).astype(F32) * (2.0 * np.pi / n)
    scale = 1.0 / np.sqrt(n)
    return (jnp.cos(phase) * scale).astype(BF16), (jnp.sin(phase) * scale).astype(BF16)


def _fourier_kernel(u_ref, cl_ref, sl_ref, cc_ref, sc_ref, w_ref, o_ref):
    u = u_ref[...]
    zc = jnp.dot(u, cc_ref[...], preferred_element_type=F32).astype(BF16)
    zs = jnp.dot(u, sc_ref[...], preferred_element_type=F32).astype(BF16)
    f = (jnp.dot(cl_ref[...], zc, preferred_element_type=F32)
         - jnp.dot(sl_ref[...], zs, preferred_element_type=F32))
    o_ref[...] = jnp.dot(f.astype(BF16), w_ref[...], preferred_element_type=F32).astype(o_ref.dtype)


def fourier_mix(u, w_four, seq_len):
    m = u.shape[0]
    groups, gw, _ = w_four.shape
    cl, sl = _dft_matrices(seq_len)
    cc, sc = _dft_matrices(gw)
    const = lambda r, c: pl.BlockSpec((r, c), lambda b, g: (0, 0), pipeline_mode=pl.Buffered(1))
    return pl.pallas_call(
        _fourier_kernel,
        grid=(m // seq_len, groups),
        in_specs=[
            pl.BlockSpec((seq_len, gw), lambda b, g: (b, g)),
            const(seq_len, seq_len), const(seq_len, seq_len), const(gw, gw), const(gw, gw),
            pl.BlockSpec((None, gw, gw), lambda b, g: (g, 0, 0)),
        ],
        out_specs=pl.BlockSpec((seq_len, gw), lambda b, g: (b, g)),
        out_shape=jax.ShapeDtypeStruct((m, groups * gw), BF16),
        compiler_params=_cparams(("arbitrary", "arbitrary"), 56),
        name="fourier_mix",
    )(u, cl, sl, cc, sc, w_four)


def _pool_matrices(seq_len):
    t = jnp.arange(seq_len, dtype=jnp.int32)
    bands, inv = [], []
    for w in POOL_WINDOWS:
        lo = jnp.clip(t - w // 2, 0, seq_len)
        hi = jnp.clip(t - w // 2 + w, 0, seq_len)
        bands.append(((t[None, :] >= lo[:, None]) & (t[None, :] < hi[:, None])).astype(BF16))
        inv.append(1.0 / (hi - lo).astype(F32))
    return jnp.stack(bands), jnp.stack(inv)[:, :, None]


def _pool_kernel(u_ref, band_ref, inv_ref, w_ref, ps_ref, o_ref):
    u = u_ref[...]
    pooled = jnp.dot(band_ref[...], u, preferred_element_type=F32) * inv_ref[...] - u.astype(F32)
    y = jnp.dot(pooled.astype(BF16), w_ref[...], preferred_element_type=F32)
    o_ref[...] = (y * ps_ref[...]).astype(o_ref.dtype)


def pool_mix(u, w_pool, pool_scale, seq_len, group_off):
    m = u.shape[0]
    groups, gw, _ = w_pool.shape
    band, inv = _pool_matrices(seq_len)
    return pl.pallas_call(
        _pool_kernel,
        grid=(groups, m // seq_len),
        in_specs=[
            pl.BlockSpec((seq_len, gw), lambda g, b: (b, g + group_off)),
            pl.BlockSpec((None, seq_len, seq_len), lambda g, b: (g, 0, 0)),
            pl.BlockSpec((None, seq_len, 1), lambda g, b: (g, 0, 0)),
            pl.BlockSpec((None, gw, gw), lambda g, b: (g, 0, 0)),
            pl.BlockSpec((1, gw), lambda g, b: (0, g)),
        ],
        out_specs=pl.BlockSpec((seq_len, gw), lambda g, b: (b, g)),
        out_shape=jax.ShapeDtypeStruct((m, groups * gw), BF16),
        compiler_params=_cparams(("arbitrary", "arbitrary"), 48),
        name="pool_mix",
    )(u, band, inv, w_pool, pool_scale.reshape(1, groups * gw))


_NT_DIMS = (((1,), (1,)), ((), ()))


def _ctx_attn_kernel(q_ref, k_ref, v_ref, o_ref):
    n_heads, _, dh = q_ref.shape
    for h in range(n_heads):
        k = k_ref[h].astype(BF16)
        v = v_ref[h].astype(BF16)
        s = lax.dot_general(q_ref[h], k, _NT_DIMS, preferred_element_type=F32)
        p = jnp.exp(s - jnp.max(s, axis=-1, keepdims=True))
        denom = jnp.sum(p, axis=-1, keepdims=True)
        o = jnp.dot(p.astype(BF16), v, preferred_element_type=F32) / denom
        o_ref[:, h * dh:(h + 1) * dh] = o.astype(o_ref.dtype)


def context_attention(q, k, v):
    b, n_heads, seq_len, dh = q.shape
    blk = pl.BlockSpec((None, n_heads, seq_len, dh), lambda i: (i, 0, 0, 0))
    return pl.pallas_call(
        _ctx_attn_kernel,
        grid=(b,),
        in_specs=[blk, blk, blk],
        out_specs=pl.BlockSpec((seq_len, n_heads * dh), lambda i: (i, 0)),
        out_shape=jax.ShapeDtypeStruct((b * seq_len, n_heads * dh), BF16),
        compiler_params=_cparams(("arbitrary",), 48),
        name="context_attention",
    )(q, k, v)


def _na_structure(rows):
    kr = min(NA_ROWS, rows)
    assert rows % NA_TILE_ROWS == 0 and rows >= NA_WIN_ROWS
    n_tiles = rows // NA_TILE_ROWS
    starts = np.clip(np.arange(n_tiles) * NA_TILE_ROWS - kr // 2, 0, rows - NA_WIN_ROWS)
    patterns, pattern_of = [], []
    for t in range(n_tiles):
        r = t * NA_TILE_ROWS + np.arange(NA_TILE_ROWS)[:, None]
        ka = starts[t] + np.arange(NA_WIN_ROWS)[None, :]
        rs = np.clip(r - kr // 2, 0, rows - kr)
        assert (rs >= starts[t]).all() and (rs + kr <= starts[t] + NA_WIN_ROWS).all()
        slot = np.where((ka >= rs) & (ka < rs + kr), ka - r + NA_ROWS - 1, 2 * NA_ROWS - 1)
        for p, known in enumerate(patterns):
            if (known == slot).all():
                pattern_of.append(p)
                break
        else:
            pattern_of.append(len(patterns))
            patterns.append(slot)
    return tuple(int(s) for s in starts), tuple(pattern_of), np.stack(patterns)


def _na_bias(rpb, dr_slot):
    n_heads = rpb.shape[0]
    qc = np.arange(GRID_W)[:, None]
    kc = np.arange(GRID_W)[None, :]
    col_start = np.clip(qc - NA_COLS // 2, 0, GRID_W - NA_COLS)
    col_ok = (kc >= col_start) & (kc < col_start + NA_COLS)
    dc = np.clip(kc - qc + NA_COLS - 1, 0, 2 * NA_COLS - 2)
    onehot = (dc.reshape(1, -1) == np.arange(2 * NA_COLS - 1)[:, None]).astype(np.float32)
    toep = jnp.einsum("hds,sn->hdn", rpb, jnp.asarray(onehot), precision=lax.Precision.HIGHEST)
    toep = jnp.where(jnp.asarray(col_ok)[None, None], toep.reshape(n_heads, 2 * NA_ROWS - 1, GRID_W, GRID_W), NEG)
    toep = jnp.concatenate([toep, jnp.full((n_heads, 1, GRID_W, GRID_W), NEG, F32)], axis=1)
    n_pat = dr_slot.shape[0]
    bias = toep[:, jnp.asarray(dr_slot)]
    bias = jnp.transpose(bias, (0, 1, 2, 4, 3, 5))
    return bias.reshape(n_heads, n_pat, NA_TILE_ROWS * GRID_W, NA_WIN_ROWS * GRID_W)


def _na_kernel(q_ref, k_ref, v_ref, ck_ref, cv_ref, bias_ref, o_ref, *, starts, pattern_of):
    tq = NA_TILE_ROWS * GRID_W
    tk = NA_WIN_ROWS * GRID_W
    ck = ck_ref[...].astype(BF16)
    cv = cv_ref[...].astype(BF16)
    for t, (start, pat) in enumerate(zip(starts, pattern_of)):
        q = q_ref[t * tq:(t + 1) * tq, :]
        kw = k_ref[start * GRID_W:start * GRID_W + tk, :]
        vw = v_ref[start * GRID_W:start * GRID_W + tk, :]
        s_loc = lax.dot_general(q, kw, _NT_DIMS, preferred_element_type=F32) + bias_ref[pat]
        s_ctx = lax.dot_general(q, ck, _NT_DIMS, preferred_element_type=F32)
        mx = jnp.maximum(jnp.max(s_loc, axis=-1, keepdims=True), jnp.max(s_ctx, axis=-1, keepdims=True))
        p_loc = jnp.exp(s_loc - mx)
        p_ctx = jnp.exp(s_ctx - mx)
        denom = jnp.sum(p_loc, axis=-1, keepdims=True) + jnp.sum(p_ctx, axis=-1, keepdims=True)
        o = (jnp.dot(p_loc.astype(BF16), vw, preferred_element_type=F32)
             + jnp.dot(p_ctx.astype(BF16), cv, preferred_element_type=F32)) / denom
        o_ref[t * tq:(t + 1) * tq, :] = o.astype(o_ref.dtype)


def neighbourhood_attention(q, k, v, ctx_k, ctx_v, layer, rpb):
    b, n_heads, seq_len, dh = q.shape
    lc = ctx_k.shape[3]
    starts, pattern_of, dr_slot = _na_structure(seq_len // GRID_W)
    bias = _na_bias(rpb, dr_slot)
    _, n_pat, tq, tk = bias.shape
    qkv = pl.BlockSpec((None, None, seq_len, dh), lambda h, i: (i, h, 0, 0))
    ctx = pl.BlockSpec((None, None, None, lc, dh), lambda h, i: (i, layer, h, 0, 0))
    return pl.pallas_call(
        functools.partial(_na_kernel, starts=starts, pattern_of=pattern_of),
        grid=(n_heads, b),
        in_specs=[qkv, qkv, qkv, ctx, ctx,
                  pl.BlockSpec((None, n_pat, tq, tk), lambda h, i: (h, 0, 0, 0))],
        out_specs=pl.BlockSpec((seq_len, dh), lambda h, i: (i, h)),
        out_shape=jax.ShapeDtypeStruct((b * seq_len, n_heads * dh), BF16),
        compiler_params=_cparams(("arbitrary", "arbitrary"), 48),
        name="neighbourhood_attention",
    )(q, k, v, ctx_k, ctx_v, bias)


SHIFT1, SCALE1, GATE1, SHIFT2, SCALE2, GATE2 = range(6)


def _trunk(x, seq_len, mod, row_of, ctx_kv, wts, alpha):
    (ln1_g, ln1_b, ln2_g, ln2_b, w_in, w_four, w_pool, pool_scale, w_out_a, w_qkv, rpb, w_out_c,
     w_up, conv_w, conv_b, w_down) = wts
    depth = mod.shape[0]
    d = x.shape[1]
    new_k, new_v = [], []
    h = ln_modulate(x, mod[0], row_of, SCALE1, SHIFT1)
    for i in range(depth):
        j = i // 2
        if i % 2 == 0:
            u = matmul([h], w_in, j, n=d)
            ya = fourier_mix(u, w_four[j], seq_len)
            yb = pool_mix(u, w_pool[j], pool_scale[j], seq_len, w_four.shape[1])
            y = matmul([ya, yb], w_out_a, j, n=d)
        else:
            q = matmul([h], w_qkv, j, n=d, col_off=0, scale=HEAD_DIM ** -0.5, seq_len=seq_len)
            if ctx_kv is None:
                k = matmul([h], w_qkv, j, n=d, col_off=d, out_dtype=F32, seq_len=seq_len)
                v = matmul([h], w_qkv, j, n=d, col_off=2 * d, out_dtype=F32, seq_len=seq_len)
                new_k.append(k)
                new_v.append(v)
                o = context_attention(q, k, v)
            else:
                k = matmul([h], w_qkv, j, n=d, col_off=d, seq_len=seq_len)
                v = matmul([h], w_qkv, j, n=d, col_off=2 * d, seq_len=seq_len)
                o = neighbourhood_attention(q, k, v, ctx_kv[0], ctx_kv[1], j, rpb[j])
            y = matmul([o], w_out_c, j, n=d)
        x, h = residual_ln(x, y, mod[i], row_of, GATE1, ln1_g[i], ln1_b[i], alpha, nxt=(mod[i], SCALE2, SHIFT2))
        f = matmul_ksplit(ffn_up(h, w_up, conv_w, conv_b, i, seq_len), w_down, i)
        nxt = (mod[i + 1], SCALE1, SHIFT1) if i + 1 < depth else None
        x, h = residual_ln(x, f, mod[i], row_of, GATE2, ln2_g[i], ln2_b[i], alpha, nxt=nxt)
    return x, new_k, new_v


def kernel(x_prompt, x_sample, cache_k, cache_v, c, c_ctx, w_ada, b_ada, ln1_g, ln1_b, ln2_g, ln2_b,
           w_in, w_four, w_pool, pool_scale, w_out_a, w_qkv, rpb, w_out_c, w_up, conv_w, conv_b, w_down):
    batch, seq, d = x_prompt.shape
    dec_batch, dec_seq, _ = x_sample.shape
    depth = w_ada.shape[0]
    alpha = float((2 * depth) ** 0.25)
    assert dec_batch < COND_ROWS

    cond = jnp.concatenate([c, c_ctx[None, :], jnp.zeros((COND_ROWS - dec_batch - 1, d), F32)], axis=0)
    mod = ada_modulation(cond, w_ada, b_ada).reshape(depth, COND_ROWS, 1, 6 * d)

    wts = (ln1_g, ln1_b, ln2_g, ln2_b,
           w_in.astype(BF16), w_four.astype(BF16), w_pool.astype(BF16), pool_scale, w_out_a.astype(BF16),
           w_qkv.astype(BF16), rpb, w_out_c.astype(BF16),
           w_up.astype(BF16), conv_w, conv_b, w_down.astype(BF16))

    y_prompt, ks, vs = _trunk(x_prompt.reshape(batch * seq, d), seq, mod, lambda r: dec_batch, None, wts, alpha)
    y_sample, _, _ = _trunk(x_sample.reshape(dec_batch * dec_seq, d), dec_seq, mod, lambda r: r // dec_seq,
                            (cache_k, cache_v), wts, alpha)
    return (y_prompt.reshape(batch, seq, d), y_sample.reshape(dec_batch, dec_seq, d),
            jnp.stack(ks, axis=1), jnp.stack(vs, axis=1))
```

```python
import functools

import numpy as np
import jax
import jax.numpy as jnp
from jax import lax
from jax.experimental import pallas as pl
from jax.experimental.pallas import tpu as pltpu

F32 = jnp.float32
BF16 = jnp.bfloat16

LN_EPS = 1e-5
NEG = -1e30
GRID_W = 64
NA_ROWS = 8
NA_COLS = 16
HEAD_DIM = 128
POOL_WINDOWS = (2, 4, 8, 16)
COND_ROWS = 16
NA_TILE_ROWS = 4
NA_WIN_ROWS = 12
LN_CHUNK = 32
MIB = 1024 * 1024


def _cparams(semantics, vmem_mib):
    return pltpu.CompilerParams(dimension_semantics=semantics, vmem_limit_bytes=int(vmem_mib * MIB))


def _tile(dim, pref):
    t = min(dim, pref)
    assert dim % t == 0, (dim, pref)
    return t


def _ada_kernel(c_ref, w_ref, b_ref, o_ref):
    c = c_ref[...]
    s = (c * jax.nn.sigmoid(c)).astype(BF16)
    o_ref[...] = jnp.dot(s, w_ref[...].astype(BF16), preferred_element_type=F32) + b_ref[...]


def ada_modulation(cond, w_ada, b_ada):
    depth, d, n = w_ada.shape
    tn = _tile(n, 512)
    return pl.pallas_call(
        _ada_kernel,
        grid=(depth, n // tn),
        in_specs=[
            pl.BlockSpec((COND_ROWS, d), lambda l, j: (0, 0)),
            pl.BlockSpec((None, d, tn), lambda l, j: (l, 0, j)),
            pl.BlockSpec((None, 1, tn), lambda l, j: (l, 0, j)),
        ],
        out_specs=pl.BlockSpec((None, COND_ROWS, tn), lambda l, j: (l, 0, j)),
        out_shape=jax.ShapeDtypeStruct((depth, COND_ROWS, n), F32),
        compiler_params=_cparams(("arbitrary", "arbitrary"), 40),
        name="ada_modulation",
    )(cond, w_ada, b_ada.reshape(depth, 1, n))


def _normalise(x):
    mu = jnp.mean(x, axis=-1, keepdims=True)
    xc = x - mu
    var = jnp.mean(xc * xc, axis=-1, keepdims=True)
    return xc * lax.rsqrt(var + LN_EPS)


def _row_chunks(n_rows, body):
    chunk = min(LN_CHUNK, n_rows)
    assert n_rows % chunk == 0

    def step(c, carry):
        body(pl.ds(pl.multiple_of(c * chunk, chunk), chunk))
        return carry

    lax.fori_loop(0, n_rows // chunk, step, 0)


def _ln_mod_kernel(x_ref, sc_ref, sh_ref, h_ref):
    scale = 1.0 + sc_ref[...]
    shift = sh_ref[...]

    def body(rows):
        h_ref[rows, :] = (_normalise(x_ref[rows, :]) * scale + shift).astype(h_ref.dtype)

    _row_chunks(x_ref.shape[0], body)


def _mod_spec(d, row_of_tile, which):
    return pl.BlockSpec((None, 1, d), lambda i: (row_of_tile(i), 0, which))


def ln_modulate(x, mod, row_of, which_scale, which_shift):
    m, d = x.shape
    tm = _tile(m, 512)
    row = lambda i: row_of(i * tm)
    return pl.pallas_call(
        _ln_mod_kernel,
        grid=(m // tm,),
        in_specs=[pl.BlockSpec((tm, d), lambda i: (i, 0)),
                  _mod_spec(d, row, which_scale), _mod_spec(d, row, which_shift)],
        out_specs=pl.BlockSpec((tm, d), lambda i: (i, 0)),
        out_shape=jax.ShapeDtypeStruct((m, d), BF16),
        compiler_params=_cparams(("arbitrary",), 40),
        name="ln_modulate",
    )(x, mod, mod)


def _res_ln_kernel(*refs, alpha, with_next):
    if with_next:
        x_ref, y_ref, g_ref, gam_ref, bet_ref, sc_ref, sh_ref, xo_ref, h_ref = refs
        scale = 1.0 + sc_ref[...]
        shift = sh_ref[...]
    else:
        x_ref, y_ref, g_ref, gam_ref, bet_ref, xo_ref = refs
    gate, gamma, beta = g_ref[...], gam_ref[...], bet_ref[...]

    def body(rows):
        z = alpha * x_ref[rows, :] + gate * y_ref[rows, :].astype(F32)
        xn = _normalise(z) * gamma + beta
        xo_ref[rows, :] = xn
        if with_next:
            h_ref[rows, :] = (_normalise(xn) * scale + shift).astype(h_ref.dtype)

    _row_chunks(x_ref.shape[0], body)


def residual_ln(x, y, mod, row_of, which_gate, gamma, beta, alpha, nxt=None):
    m, d = x.shape
    tm = _tile(m, 256)
    row = lambda i: row_of(i * tm)
    tile = pl.BlockSpec((tm, d), lambda i: (i, 0))
    vec = pl.BlockSpec((1, d), lambda i: (0, 0))
    in_specs = [tile, tile, _mod_spec(d, row, which_gate), vec, vec]
    args = [x, y, mod, gamma.reshape(1, d), beta.reshape(1, d)]
    out_specs = [tile]
    out_shape = [jax.ShapeDtypeStruct((m, d), F32)]
    if nxt is not None:
        in_specs += [_mod_spec(d, row, nxt[1]), _mod_spec(d, row, nxt[2])]
        args += [nxt[0], nxt[0]]
        out_specs.append(tile)
        out_shape.append(jax.ShapeDtypeStruct((m, d), BF16))
    out = pl.pallas_call(
        functools.partial(_res_ln_kernel, alpha=alpha, with_next=nxt is not None),
        grid=(m // tm,),
        in_specs=in_specs,
        out_specs=out_specs,
        out_shape=out_shape,
        compiler_params=_cparams(("arbitrary",), 48),
        name="residual_ln",
    )(*args)
    return (out[0], out[1]) if nxt is not None else (out[0], None)


def _mm_kernel(*refs, n_parts, scale, head_major):
    a_refs, w_refs, o_ref = refs[:n_parts], refs[n_parts:2 * n_parts], refs[2 * n_parts]
    acc = None
    for a_ref, w_ref in zip(a_refs, w_refs):
        part = jnp.dot(a_ref[...], w_ref[...], preferred_element_type=F32)
        acc = part if acc is None else acc + part
    if scale is not None:
        acc = acc * scale
    if not head_major:
        o_ref[...] = acc.astype(o_ref.dtype)
        return
    bt, nh, lb, dh = o_ref.shape
    for b in range(bt):
        for h in range(nh):
            o_ref[b, h] = acc[b * lb:(b + 1) * lb, h * dh:(h + 1) * dh].astype(o_ref.dtype)


def matmul(a_parts, w, layer, *, n, col_off=0, out_dtype=BF16, scale=None, seq_len=None, tm=1024, tn=1024):
    m, kp = a_parts[0].shape
    assert all(a.shape == (m, kp) for a in a_parts) and w.shape[1] == kp * len(a_parts)
    tm, tn = _tile(m, tm), _tile(n, tn)
    assert col_off % tn == 0
    joff = col_off // tn
    in_specs = [pl.BlockSpec((tm, kp), lambda i, j: (i, 0)) for _ in a_parts]
    in_specs += [pl.BlockSpec((None, kp, tn), functools.partial(lambda i, j, p: (layer, p, j + joff), p=p))
                 for p in range(len(a_parts))]
    if seq_len is None:
        out_spec = pl.BlockSpec((tm, tn), lambda i, j: (i, j))
        out_shape = jax.ShapeDtypeStruct((m, n), out_dtype)
    else:
        nh = tn // HEAD_DIM
        out_shape = jax.ShapeDtypeStruct((m // seq_len, n // HEAD_DIM, seq_len, HEAD_DIM), out_dtype)
        if tm >= seq_len:
            out_spec = pl.BlockSpec((tm // seq_len, nh, seq_len, HEAD_DIM), lambda i, j: (i, j, 0, 0))
        else:
            per = seq_len // tm
            out_spec = pl.BlockSpec((1, nh, tm, HEAD_DIM), lambda i, j: (i // per, j, i % per, 0))
    return pl.pallas_call(
        functools.partial(_mm_kernel, n_parts=len(a_parts), scale=scale, head_major=seq_len is not None),
        grid=(m // tm, n // tn),
        in_specs=in_specs,
        out_specs=out_spec,
        out_shape=out_shape,
        compiler_params=_cparams(("arbitrary", "arbitrary"), 56),
        name="matmul",
    )(*a_parts, *([w] * len(a_parts)))


def _mm_ksplit_kernel(a_ref, w_ref, o_ref, acc_ref, *, k_steps):
    part = jnp.dot(a_ref[...], w_ref[...], preferred_element_type=F32)
    s = pl.program_id(2)

    @pl.when(s == 0)
    def _():
        acc_ref[...] = part

    @pl.when((s > 0) & (s < k_steps - 1))
    def _():
        acc_ref[...] += part

    @pl.when(s == k_steps - 1)
    def _():
        o_ref[...] = (acc_ref[...] + part).astype(o_ref.dtype)


def matmul_ksplit(a, w, layer, *, out_dtype=BF16, tm=1024, tn=512, k_steps=2):
    m, k = a.shape
    n = w.shape[2]
    tm, tn = _tile(m, tm), _tile(n, tn)
    tk = k // k_steps
    assert k_steps >= 2 and tk * k_steps == k and tk % 128 == 0
    return pl.pallas_call(
        functools.partial(_mm_ksplit_kernel, k_steps=k_steps),
        grid=(m // tm, n // tn, k_steps),
        in_specs=[pl.BlockSpec((tm, tk), lambda i, j, s: (i, s)),
                  pl.BlockSpec((None, tk, tn), lambda i, j, s: (layer, s, j))],
        out_specs=pl.BlockSpec((tm, tn), lambda i, j, s: (i, j)),
        out_shape=jax.ShapeDtypeStruct((m, n), out_dtype),
        scratch_shapes=[pltpu.VMEM((tm, tn), F32)],
        compiler_params=_cparams(("arbitrary", "arbitrary", "arbitrary"), 56),
        name="matmul_ksplit",
    )(a, w)


def _ffn_up_kernel(a_ref, halo_ref, wa_ref, wg_ref, cwa_ref, cwg_ref, o_ref, *, seq_len):
    a = a_ref[...]
    halo = halo_ref[...]
    tm, tn = o_ref.shape
    sub = lax.broadcasted_iota(jnp.int32, (8, tn), 0)
    inner = range(seq_len, tm, seq_len)

    def patch(x, group, row, value):
        lo, hi = group * 8, (group + 1) * 8
        parts = [x[:lo]] * (lo > 0) + [jnp.where(sub == row, value, x[lo:hi])] + [x[hi:]] * (hi < tm)
        return jnp.concatenate(parts, axis=0)

    def conv(w_ref, cw_ref):
        w = w_ref[...]
        u = jnp.dot(a, w, preferred_element_type=F32)
        uh = jnp.dot(halo, w, preferred_element_type=F32)
        prev = patch(pltpu.roll(u, 1, 0), 0, 0, uh[7:8])
        nxt = patch(pltpu.roll(u, tm - 1, 0), tm // 8 - 1, 7, uh[0:1])
        for r in inner:
            prev = patch(prev, r // 8, 0, 0.0)
            nxt = patch(nxt, r // 8 - 1, 7, 0.0)
        cw = cw_ref[...]
        return prev * cw[0:1] + u * cw[1:2] + nxt * cw[2:3] + cw[3:4]

    act = conv(wa_ref, cwa_ref)
    half = 0.5 * conv(wg_ref, cwg_ref)
    o_ref[...] = ((half + half * jnp.tanh(half)) * act).astype(o_ref.dtype)


def ffn_up(h, w_up, conv_w, conv_b, layer, seq_len, *, tm=1024, tn=256):
    m, d = h.shape
    d_ff = w_up.shape[2] // 2
    tm, tn = _tile(m, tm), _tile(d_ff, tn)
    assert (tm % seq_len == 0 or seq_len % tm == 0) and tm % 16 == 0
    nt, nj = m // tm, d_ff // tn
    tiles = h.reshape(nt, tm, d)
    zero = jnp.zeros((1, d), h.dtype)
    starts = np.arange(nt) * tm
    before = jnp.concatenate([zero, tiles[:-1, -1]], axis=0)
    before = jnp.where(jnp.asarray(starts % seq_len == 0)[:, None], 0, before)
    after = jnp.concatenate([tiles[1:, 0], zero], axis=0)
    after = jnp.where(jnp.asarray((starts + tm) % seq_len == 0)[:, None], 0, after)
    halo = jnp.concatenate([after[:, None], jnp.zeros((nt, 6, d), h.dtype), before[:, None]], axis=1)
    taps = jnp.concatenate([conv_w[layer], conv_b[layer][None]], axis=0)
    return pl.pallas_call(
        functools.partial(_ffn_up_kernel, seq_len=seq_len),
        grid=(nt, nj),
        in_specs=[
            pl.BlockSpec((tm, d), lambda i, j: (i, 0)),
            pl.BlockSpec((8, d), lambda i, j: (i, 0)),
            pl.BlockSpec((None, d, tn), lambda i, j: (layer, 0, j)),
            pl.BlockSpec((None, d, tn), lambda i, j: (layer, 0, j + nj)),
            pl.BlockSpec((4, tn), lambda i, j: (0, j)),
            pl.BlockSpec((4, tn), lambda i, j: (0, j + nj)),
        ],
        out_specs=pl.BlockSpec((tm, tn), lambda i, j: (i, j)),
        out_shape=jax.ShapeDtypeStruct((m, d_ff), BF16),
        compiler_params=_cparams(("arbitrary", "arbitrary"), 48),
        name="ffn_up",
    )(h, halo.reshape(nt * 8, d), w_up, w_up, taps, taps)


def _dft_matrices(n):
    k = jnp.arange(n, dtype=jnp.int32)
    phase = ((k[:, None] * k[None, :]) % n).astype(F32) * (2.0 * np.pi / n)
    scale = 1.0 / np.sqrt(n)
    return (jnp.cos(phase) * scale).astype(BF16), (jnp.sin(phase) * scale).astype(BF16)


def _fourier_kernel(u_ref, cl_ref, sl_ref, cc_ref, sc_ref, w_ref, o_ref):
    u = u_ref[...]
    zc = jnp.dot(u, cc_ref[...], preferred_element_type=F32).astype(BF16)
    zs = jnp.dot(u, sc_ref[...], preferred_element_type=F32).astype(BF16)
    f = (jnp.dot(cl_ref[...], zc, preferred_element_type=F32)
         - jnp.dot(sl_ref[...], zs, preferred_element_type=F32))
    o_ref[...] = jnp.dot(f.astype(BF16), w_ref[...], preferred_element_type=F32).astype(o_ref.dtype)


def fourier_mix(u, w_four, seq_len):
    m = u.shape[0]
    groups, gw, _ = w_four.shape
    cl, sl = _dft_matrices(seq_len)
    cc, sc = _dft_matrices(gw)
    const = lambda r, c: pl.BlockSpec((r, c), lambda b, g: (0, 0), pipeline_mode=pl.Buffered(1))
    return pl.pallas_call(
        _fourier_kernel,
        grid=(m // seq_len, groups),
        in_specs=[
            pl.BlockSpec((seq_len, gw), lambda b, g: (b, g)),
            const(seq_len, seq_len), const(seq_len, seq_len), const(gw, gw), const(gw, gw),
            pl.BlockSpec((None, gw, gw), lambda b, g: (g, 0, 0)),
        ],
        out_specs=pl.BlockSpec((seq_len, gw), lambda b, g: (b, g)),
        out_shape=jax.ShapeDtypeStruct((m, groups * gw), BF16),
        compiler_params=_cparams(("arbitrary", "arbitrary"), 56),
        name="fourier_mix",
    )(u, cl, sl, cc, sc, w_four)


def _pool_matrices(seq_len):
    t = jnp.arange(seq_len, dtype=jnp.int32)
    bands, inv = [], []
    for w in POOL_WINDOWS:
        lo = jnp.clip(t - w // 2, 0, seq_len)
        hi = jnp.clip(t - w // 2 + w, 0, seq_len)
        bands.append(((t[None, :] >= lo[:, None]) & (t[None, :] < hi[:, None])).astype(BF16))
        inv.append(1.0 / (hi - lo).astype(F32))
    return jnp.stack(bands), jnp.stack(inv)[:, :, None]


def _pool_kernel(u_ref, band_ref, inv_ref, w_ref, ps_ref, o_ref):
    u = u_ref[...]
    pooled = jnp.dot(band_ref[...], u, preferred_element_type=F32) * inv_ref[...] - u.astype(F32)
    y = jnp.dot(pooled.astype(BF16), w_ref[...], preferred_element_type=F32)
    o_ref[...] = (y * ps_ref[...]).astype(o_ref.dtype)


def pool_mix(u, w_pool, pool_scale, seq_len, group_off):
    m = u.shape[0]
    groups, gw, _ = w_pool.shape
    band, inv = _pool_matrices(seq_len)
    return pl.pallas_call(
        _pool_kernel,
        grid=(groups, m // seq_len),
        in_specs=[
            pl.BlockSpec((seq_len, gw), lambda g, b: (b, g + group_off)),
            pl.BlockSpec((None, seq_len, seq_len), lambda g, b: (g, 0, 0)),
            pl.BlockSpec((None, seq_len, 1), lambda g, b: (g, 0, 0)),
            pl.BlockSpec((None, gw, gw), lambda g, b: (g, 0, 0)),
            pl.BlockSpec((1, gw), lambda g, b: (0, g)),
        ],
        out_specs=pl.BlockSpec((seq_len, gw), lambda g, b: (b, g)),
        out_shape=jax.ShapeDtypeStruct((m, groups * gw), BF16),
        compiler_params=_cparams(("arbitrary", "arbitrary"), 48),
        name="pool_mix",
    )(u, band, inv, w_pool, pool_scale.reshape(1, groups * gw))


_NT_DIMS = (((1,), (1,)), ((), ()))


def _ctx_attn_kernel(q_ref, k_ref, v_ref, o_ref):
    n_heads, _, dh = q_ref.shape
    for h in range(n_heads):
        k = k_ref[h].astype(BF16)
        v = v_ref[h].astype(BF16)
        s = lax.dot_general(q_ref[h], k, _NT_DIMS, preferred_element_type=F32)
        p = jnp.exp(s - jnp.max(s, axis=-1, keepdims=True))
        denom = jnp.sum(p, axis=-1, keepdims=True)
        o = jnp.dot(p.astype(BF16), v, preferred_element_type=F32) / denom
        o_ref[:, h * dh:(h + 1) * dh] = o.astype(o_ref.dtype)


def context_attention(q, k, v):
    b, n_heads, seq_len, dh = q.shape
    blk = pl.BlockSpec((None, n_heads, seq_len, dh), lambda i: (i, 0, 0, 0))
    return pl.pallas_call(
        _ctx_attn_kernel,
        grid=(b,),
        in_specs=[blk, blk, blk],
        out_specs=pl.BlockSpec((seq_len, n_heads * dh), lambda i: (i, 0)),
        out_shape=jax.ShapeDtypeStruct((b * seq_len, n_heads * dh), BF16),
        compiler_params=_cparams(("arbitrary",), 48),
        name="context_attention",
    )(q, k, v)


def _na_structure(rows):
    kr = min(NA_ROWS, rows)
    assert rows % NA_TILE_ROWS == 0 and rows >= NA_WIN_ROWS
    n_tiles = rows // NA_TILE_ROWS
    starts = np.clip(np.arange(n_tiles) * NA_TILE_ROWS - kr // 2, 0, rows - NA_WIN_ROWS)
    patterns, pattern_of = [], []
    for t in range(n_tiles):
        r = t * NA_TILE_ROWS + np.arange(NA_TILE_ROWS)[:, None]
        ka = starts[t] + np.arange(NA_WIN_ROWS)[None, :]
        rs = np.clip(r - kr // 2, 0, rows - kr)
        assert (rs >= starts[t]).all() and (rs + kr <= starts[t] + NA_WIN_ROWS).all()
        slot = np.where((ka >= rs) & (ka < rs + kr), ka - r + NA_ROWS - 1, 2 * NA_ROWS - 1)
        for p, known in enumerate(patterns):
            if (known == slot).all():
                pattern_of.append(p)
                break
        else:
            pattern_of.append(len(patterns))
            patterns.append(slot)
    return tuple(int(s) for s in starts), tuple(pattern_of), np.stack(patterns)


def _na_bias(rpb, dr_slot):
    n_heads = rpb.shape[0]
    qc = np.arange(GRID_W)[:, None]
    kc = np.arange(GRID_W)[None, :]
    col_start = np.clip(qc - NA_COLS // 2, 0, GRID_W - NA_COLS)
    col_ok = (kc >= col_start) & (kc < col_start + NA_COLS)
    dc = np.clip(kc - qc + NA_COLS - 1, 0, 2 * NA_COLS - 2)
    onehot = (dc.reshape(1, -1) == np.arange(2 * NA_COLS - 1)[:, None]).astype(np.float32)
    toep = jnp.einsum("hds,sn->hdn", rpb, jnp.asarray(onehot), precision=lax.Precision.HIGHEST)
    toep = jnp.where(jnp.asarray(col_ok)[None, None], toep.reshape(n_heads, 2 * NA_ROWS - 1, GRID_W, GRID_W), NEG)
    toep = jnp.concatenate([toep, jnp.full((n_heads, 1, GRID_W, GRID_W), NEG, F32)], axis=1)
    n_pat = dr_slot.shape[0]
    bias = toep[:, jnp.asarray(dr_slot)]
    bias = jnp.transpose(bias, (0, 1, 2, 4, 3, 5))
    return bias.reshape(n_heads, n_pat, NA_TILE_ROWS * GRID_W, NA_WIN_ROWS * GRID_W)


def _na_kernel(q_ref, k_ref, v_ref, ck_ref, cv_ref, bias_ref, o_ref, *, starts, pattern_of):
    tq = NA_TILE_ROWS * GRID_W
    tk = NA_WIN_ROWS * GRID_W
    ck = ck_ref[...].astype(BF16)
    cv = cv_ref[...].astype(BF16)

    def scores(t):
        q = q_ref[t * tq:(t + 1) * tq, :]
        kw = k_ref[starts[t] * GRID_W:starts[t] * GRID_W + tk, :]
        s_loc = lax.dot_general(q, kw, _NT_DIMS, preferred_element_type=F32) + bias_ref[pattern_of[t]]
        s_ctx = lax.dot_general(q, ck, _NT_DIMS, preferred_element_type=F32)
        return s_loc, s_ctx

    def finish(t, s_loc, s_ctx):
        vw = v_ref[starts[t] * GRID_W:starts[t] * GRID_W + tk, :]
        mx = jnp.maximum(jnp.max(s_loc, axis=-1, keepdims=True), jnp.max(s_ctx, axis=-1, keepdims=True))
        p_loc = jnp.exp(s_loc - mx)
        p_ctx = jnp.exp(s_ctx - mx)
        denom = jnp.sum(p_loc, axis=-1, keepdims=True) + jnp.sum(p_ctx, axis=-1, keepdims=True)
        o = (jnp.dot(p_loc.astype(BF16), vw, preferred_element_type=F32)
             + jnp.dot(p_ctx.astype(BF16), cv, preferred_element_type=F32)) / denom
        o_ref[t * tq:(t + 1) * tq, :] = o.astype(o_ref.dtype)

    n_tiles = len(starts)
    sc = scores(0)
    for t in range(n_tiles):
        nxt = scores(t + 1) if t + 1 < n_tiles else None
        finish(t, *sc)
        sc = nxt


def neighbourhood_attention(q, k, v, ctx_k, ctx_v, layer, rpb):
    b, n_heads, seq_len, dh = q.shape
    lc = ctx_k.shape[3]
    starts, pattern_of, dr_slot = _na_structure(seq_len // GRID_W)
    bias = _na_bias(rpb, dr_slot)
    _, n_pat, tq, tk = bias.shape
    qkv = pl.BlockSpec((None, None, seq_len, dh), lambda h, i: (i, h, 0, 0))
    ctx = pl.BlockSpec((None, None, None, lc, dh), lambda h, i: (i, layer, h, 0, 0))
    return pl.pallas_call(
        functools.partial(_na_kernel, starts=starts, pattern_of=pattern_of),
        grid=(n_heads, b),
        in_specs=[qkv, qkv, qkv, ctx, ctx,
                  pl.BlockSpec((None, n_pat, tq, tk), lambda h, i: (h, 0, 0, 0))],
        out_specs=pl.BlockSpec((seq_len, dh), lambda h, i: (i, h)),
        out_shape=jax.ShapeDtypeStruct((b * seq_len, n_heads * dh), BF16),
        compiler_params=_cparams(("arbitrary", "arbitrary"), 48),
        name="neighbourhood_attention",
    )(q, k, v, ctx_k, ctx_v, bias)


SHIFT1, SCALE1, GATE1, SHIFT2, SCALE2, GATE2 = range(6)


def _trunk(x, seq_len, mod, row_of, ctx_kv, wts, alpha):
    (ln1_g, ln1_b, ln2_g, ln2_b, w_in, w_four, w_pool, pool_scale, w_out_a, w_qkv, rpb, w_out_c,
     w_up, conv_w, conv_b, w_down) = wts
    depth = mod.shape[0]
    d = x.shape[1]
    new_k, new_v = [], []
    h = ln_modulate(x, mod[0], row_of, SCALE1, SHIFT1)
    for i in range(depth):
        j = i // 2
        if i % 2 == 0:
            u = matmul([h], w_in, j, n=d)
            ya = fourier_mix(u, w_four[j], seq_len)
            yb = pool_mix(u, w_pool[j], pool_scale[j], seq_len, w_four.shape[1])
            y = matmul([ya, yb], w_out_a, j, n=d)
        else:
            q = matmul([h], w_qkv, j, n=d, col_off=0, scale=HEAD_DIM ** -0.5, seq_len=seq_len)
            if ctx_kv is None:
                k = matmul([h], w_qkv, j, n=d, col_off=d, out_dtype=F32, seq_len=seq_len)
                v = matmul([h], w_qkv, j, n=d, col_off=2 * d, out_dtype=F32, seq_len=seq_len)
                new_k.append(k)
                new_v.append(v)
                o = context_attention(q, k, v)
            else:
                k = matmul([h], w_qkv, j, n=d, col_off=d, seq_len=seq_len)
                v = matmul([h], w_qkv, j, n=d, col_off=2 * d, seq_len=seq_len)
                o = neighbourhood_attention(q, k, v, ctx_kv[0], ctx_kv[1], j, rpb[j])
            y = matmul([o], w_out_c, j, n=d)
        x, h = residual_ln(x, y, mod[i], row_of, GATE1, ln1_g[i], ln1_b[i], alpha, nxt=(mod[i], SCALE2, SHIFT2))
        f = matmul_ksplit(ffn_up(h, w_up, conv_w, conv_b, i, seq_len), w_down, i)
        nxt = (mod[i + 1], SCALE1, SHIFT1) if i + 1 < depth else None
        x, h = residual_ln(x, f, mod[i], row_of, GATE2, ln2_g[i], ln2_b[i], alpha, nxt=nxt)
    return x, new_k, new_v


def kernel(x_prompt, x_sample, cache_k, cache_v, c, c_ctx, w_ada, b_ada, ln1_g, ln1_b, ln2_g, ln2_b,
           w_in, w_four, w_pool, pool_scale, w_out_a, w_qkv, rpb, w_out_c, w_up, conv_w, conv_b, w_down):
    batch, seq, d = x_prompt.shape
    dec_batch, dec_seq, _ = x_sample.shape
    depth = w_ada.shape[0]
    alpha = float((2 * depth) ** 0.25)
    assert dec_batch < COND_ROWS

    cond = jnp.concatenate([c, c_ctx[None, :], jnp.zeros((COND_ROWS - dec_batch - 1, d), F32)], axis=0)
    mod = ada_modulation(cond, w_ada, b_ada).reshape(depth, COND_ROWS, 1, 6 * d)

    wts = (ln1_g, ln1_b, ln2_g, ln2_b,
           w_in.astype(BF16), w_four.astype(BF16), w_pool.astype(BF16), pool_scale, w_out_a.astype(BF16),
           w_qkv.astype(BF16), rpb, w_out_c.astype(BF16),
           w_up.astype(BF16), conv_w, conv_b, w_down.astype(BF16))

    y_prompt, ks, vs = _trunk(x_prompt.reshape(batch * seq, d), seq, mod, lambda r: dec_batch, None, wts, alpha)
    y_sample, _, _ = _trunk(x_sample.reshape(dec_batch * dec_seq, d), dec_seq, mod, lambda r: r // dec_seq,
                            (cache_k, cache_v), wts, alpha)
    return (y_prompt.reshape(batch, seq, d), y_sample.reshape(dec_batch, dec_seq, d),
            jnp.stack(ks, axis=1), jnp.stack(vs, axis=1))
```

```python
import functools

import numpy as np
import jax
import jax.numpy as jnp
from jax import lax
from jax.experimental import pallas as pl
from jax.experimental.pallas import tpu as pltpu

F32 = jnp.float32
BF16 = jnp.bfloat16

LN_EPS = 1e-5
NEG = -1e30
GRID_W = 64
NA_ROWS = 8
NA_COLS = 16
HEAD_DIM = 128
POOL_WINDOWS = (2, 4, 8, 16)
COND_ROWS = 16
NA_TILE_ROWS = 4
NA_WIN_ROWS = 12
LN_CHUNK = 64
POOL_ROWS = 256
MIB = 1024 * 1024


def _cparams(semantics, vmem_mib):
    return pltpu.CompilerParams(dimension_semantics=semantics, vmem_limit_bytes=int(vmem_mib * MIB))


def _tile(dim, pref):
    t = min(dim, pref)
    assert dim % t == 0, (dim, pref)
    return t


def _ada_kernel(c_ref, w_ref, b_ref, o_ref):
    c = c_ref[...]
    s = (c * jax.nn.sigmoid(c)).astype(BF16)
    o_ref[...] = jnp.dot(s, w_ref[...].astype(BF16), preferred_element_type=F32) + b_ref[...]


def ada_modulation(cond, w_ada, b_ada):
    depth, d, n = w_ada.shape
    tn = _tile(n, 512)
    return pl.pallas_call(
        _ada_kernel,
        grid=(depth, n // tn),
        in_specs=[
            pl.BlockSpec((COND_ROWS, d), lambda l, j: (0, 0)),
            pl.BlockSpec((None, d, tn), lambda l, j: (l, 0, j)),
            pl.BlockSpec((None, 1, tn), lambda l, j: (l, 0, j)),
        ],
        out_specs=pl.BlockSpec((None, COND_ROWS, tn), lambda l, j: (l, 0, j)),
        out_shape=jax.ShapeDtypeStruct((depth, COND_ROWS, n), F32),
        compiler_params=_cparams(("arbitrary", "arbitrary"), 40),
        name="ada_modulation",
    )(cond, w_ada, b_ada.reshape(depth, 1, n))


def _normalise(x):
    mu = jnp.mean(x, axis=-1, keepdims=True)
    xc = x - mu
    var = jnp.mean(xc * xc, axis=-1, keepdims=True)
    return xc * lax.rsqrt(var + LN_EPS)


def _row_chunks(n_rows, body):
    chunk = min(LN_CHUNK, n_rows)
    assert n_rows % chunk == 0

    def step(c, carry):
        body(pl.ds(pl.multiple_of(c * chunk, chunk), chunk))
        return carry

    lax.fori_loop(0, n_rows // chunk, step, 0)


def _ln_mod_kernel(x_ref, sc_ref, sh_ref, h_ref):
    scale = 1.0 + sc_ref[...]
    shift = sh_ref[...]

    def body(rows):
        h_ref[rows, :] = (_normalise(x_ref[rows, :]) * scale + shift).astype(h_ref.dtype)

    _row_chunks(x_ref.shape[0], body)


def _mod_spec(d, row_of_tile, which):
    return pl.BlockSpec((None, 1, d), lambda i: (row_of_tile(i), 0, which))


def ln_modulate(x, mod, row_of, which_scale, which_shift):
    m, d = x.shape
    tm = _tile(m, 512)
    row = lambda i: row_of(i * tm)
    return pl.pallas_call(
        _ln_mod_kernel,
        grid=(m // tm,),
        in_specs=[pl.BlockSpec((tm, d), lambda i: (i, 0)),
                  _mod_spec(d, row, which_scale), _mod_spec(d, row, which_shift)],
        out_specs=pl.BlockSpec((tm, d), lambda i: (i, 0)),
        out_shape=jax.ShapeDtypeStruct((m, d), BF16),
        compiler_params=_cparams(("arbitrary",), 40),
        name="ln_modulate",
    )(x, mod, mod)


def _res_ln_kernel(*refs, alpha, with_next):
    if with_next:
        x_ref, y_ref, g_ref, gam_ref, bet_ref, sc_ref, sh_ref, xo_ref, h_ref = refs
        scale = 1.0 + sc_ref[...]
        shift = sh_ref[...]
    else:
        x_ref, y_ref, g_ref, gam_ref, bet_ref, xo_ref = refs
    gate, gamma, beta = g_ref[...], gam_ref[...], bet_ref[...]

    def body(rows):
        z = alpha * x_ref[rows, :] + gate * y_ref[rows, :].astype(F32)
        xn = _normalise(z) * gamma + beta
        xo_ref[rows, :] = xn
        if with_next:
            h_ref[rows, :] = (_normalise(xn) * scale + shift).astype(h_ref.dtype)

    _row_chunks(x_ref.shape[0], body)


def residual_ln(x, y, mod, row_of, which_gate, gamma, beta, alpha, nxt=None):
    m, d = x.shape
    tm = _tile(m, 256)
    row = lambda i: row_of(i * tm)
    tile = pl.BlockSpec((tm, d), lambda i: (i, 0))
    vec = pl.BlockSpec((1, d), lambda i: (0, 0))
    in_specs = [tile, tile, _mod_spec(d, row, which_gate), vec, vec]
    args = [x, y, mod, gamma.reshape(1, d), beta.reshape(1, d)]
    out_specs = [tile]
    out_shape = [jax.ShapeDtypeStruct((m, d), F32)]
    if nxt is not None:
        in_specs += [_mod_spec(d, row, nxt[1]), _mod_spec(d, row, nxt[2])]
        args += [nxt[0], nxt[0]]
        out_specs.append(tile)
        out_shape.append(jax.ShapeDtypeStruct((m, d), BF16))
    out = pl.pallas_call(
        functools.partial(_res_ln_kernel, alpha=alpha, with_next=nxt is not None),
        grid=(m // tm,),
        in_specs=in_specs,
        out_specs=out_specs,
        out_shape=out_shape,
        compiler_params=_cparams(("arbitrary",), 48),
        name="residual_ln",
    )(*args)
    return (out[0], out[1]) if nxt is not None else (out[0], None)


def _mm_kernel(*refs, n_parts, scale, head_major):
    a_refs, w_refs, o_ref = refs[:n_parts], refs[n_parts:2 * n_parts], refs[2 * n_parts]
    acc = None
    for a_ref, w_ref in zip(a_refs, w_refs):
        part = jnp.dot(a_ref[...], w_ref[...], preferred_element_type=F32)
        acc = part if acc is None else acc + part
    if scale is not None:
        acc = acc * scale
    if not head_major:
        o_ref[...] = acc.astype(o_ref.dtype)
        return
    bt, nh, lb, dh = o_ref.shape
    for b in range(bt):
        for h in range(nh):
            o_ref[b, h] = acc[b * lb:(b + 1) * lb, h * dh:(h + 1) * dh].astype(o_ref.dtype)


def matmul(a_parts, w, layer, *, n, col_off=0, out_dtype=BF16, scale=None, seq_len=None, tm=1024, tn=1024):
    m, kp = a_parts[0].shape
    assert all(a.shape == (m, kp) for a in a_parts) and w.shape[1] == kp * len(a_parts)
    tm, tn = _tile(m, tm), _tile(n, tn)
    assert col_off % tn == 0
    joff = col_off // tn
    in_specs = [pl.BlockSpec((tm, kp), lambda i, j: (i, 0)) for _ in a_parts]
    in_specs += [pl.BlockSpec((None, kp, tn), functools.partial(lambda i, j, p: (layer, p, j + joff), p=p))
                 for p in range(len(a_parts))]
    if seq_len is None:
        out_spec = pl.BlockSpec((tm, tn), lambda i, j: (i, j))
        out_shape = jax.ShapeDtypeStruct((m, n), out_dtype)
    else:
        nh = tn // HEAD_DIM
        out_shape = jax.ShapeDtypeStruct((m // seq_len, n // HEAD_DIM, seq_len, HEAD_DIM), out_dtype)
        if tm >= seq_len:
            out_spec = pl.BlockSpec((tm // seq_len, nh, seq_len, HEAD_DIM), lambda i, j: (i, j, 0, 0))
        else:
            per = seq_len // tm
            out_spec = pl.BlockSpec((1, nh, tm, HEAD_DIM), lambda i, j: (i // per, j, i % per, 0))
    return pl.pallas_call(
        functools.partial(_mm_kernel, n_parts=len(a_parts), scale=scale, head_major=seq_len is not None),
        grid=(m // tm, n // tn),
        in_specs=in_specs,
        out_specs=out_spec,
        out_shape=out_shape,
        compiler_params=_cparams(("arbitrary", "arbitrary"), 56),
        name="matmul",
    )(*a_parts, *([w] * len(a_parts)))


def _mm_ksplit_kernel(a_ref, w_ref, o_ref, acc_ref, *, k_steps):
    part = jnp.dot(a_ref[...], w_ref[...], preferred_element_type=F32)
    s = pl.program_id(2)

    @pl.when(s == 0)
    def _():
        acc_ref[...] = part

    @pl.when((s > 0) & (s < k_steps - 1))
    def _():
        acc_ref[...] += part

    @pl.when(s == k_steps - 1)
    def _():
        o_ref[...] = (acc_ref[...] + part).astype(o_ref.dtype)


def matmul_ksplit(a, w, layer, *, out_dtype=BF16, tm=1024, tn=512, k_steps=2):
    m, k = a.shape
    n = w.shape[2]
    tm, tn = _tile(m, tm), _tile(n, tn)
    tk = k // k_steps
    assert k_steps >= 2 and tk * k_steps == k and tk % 128 == 0
    return pl.pallas_call(
        functools.partial(_mm_ksplit_kernel, k_steps=k_steps),
        grid=(m // tm, n // tn, k_steps),
        in_specs=[pl.BlockSpec((tm, tk), lambda i, j, s: (i, s)),
                  pl.BlockSpec((None, tk, tn), lambda i, j, s: (layer, s, j))],
        out_specs=pl.BlockSpec((tm, tn), lambda i, j, s: (i, j)),
        out_shape=jax.ShapeDtypeStruct((m, n), out_dtype),
        scratch_shapes=[pltpu.VMEM((tm, tn), F32)],
        compiler_params=_cparams(("arbitrary", "arbitrary", "arbitrary"), 56),
        name="matmul_ksplit",
    )(a, w)


def _ffn_up_kernel(a_ref, halo_ref, wa_ref, wg_ref, cwa_ref, cwg_ref, o_ref, *, seq_len):
    a = a_ref[...]
    halo = halo_ref[...]
    tm, tn = o_ref.shape
    sub = lax.broadcasted_iota(jnp.int32, (8, tn), 0)
    inner = range(seq_len, tm, seq_len)

    def patch(x, group, row, value):
        lo, hi = group * 8, (group + 1) * 8
        parts = [x[:lo]] * (lo > 0) + [jnp.where(sub == row, value, x[lo:hi])] + [x[hi:]] * (hi < tm)
        return jnp.concatenate(parts, axis=0)

    def conv(w_ref, cw_ref):
        w = w_ref[...]
        u = jnp.dot(a, w, preferred_element_type=F32)
        uh = jnp.dot(halo, w, preferred_element_type=F32)
        prev = patch(pltpu.roll(u, 1, 0), 0, 0, uh[7:8])
        nxt = patch(pltpu.roll(u, tm - 1, 0), tm // 8 - 1, 7, uh[0:1])
        for r in inner:
            prev = patch(prev, r // 8, 0, 0.0)
            nxt = patch(nxt, r // 8 - 1, 7, 0.0)
        cw = cw_ref[...]
        return prev * cw[0:1] + u * cw[1:2] + nxt * cw[2:3] + cw[3:4]

    act = conv(wa_ref, cwa_ref)
    half = 0.5 * conv(wg_ref, cwg_ref)
    o_ref[...] = ((half + half * jnp.tanh(half)) * act).astype(o_ref.dtype)


def ffn_up(h, w_up, conv_w, conv_b, layer, seq_len, *, tm=1024, tn=256):
    m, d = h.shape
    d_ff = w_up.shape[2] // 2
    tm, tn = _tile(m, tm), _tile(d_ff, tn)
    assert (tm % seq_len == 0 or seq_len % tm == 0) and tm % 16 == 0
    nt, nj = m // tm, d_ff // tn
    tiles = h.reshape(nt, tm, d)
    zero = jnp.zeros((1, d), h.dtype)
    starts = np.arange(nt) * tm
    before = jnp.concatenate([zero, tiles[:-1, -1]], axis=0)
    before = jnp.where(jnp.asarray(starts % seq_len == 0)[:, None], 0, before)
    after = jnp.concatenate([tiles[1:, 0], zero], axis=0)
    after = jnp.where(jnp.asarray((starts + tm) % seq_len == 0)[:, None], 0, after)
    halo = jnp.concatenate([after[:, None], jnp.zeros((nt, 6, d), h.dtype), before[:, None]], axis=1)
    taps = jnp.concatenate([conv_w[layer], conv_b[layer][None]], axis=0)
    return pl.pallas_call(
        functools.partial(_ffn_up_kernel, seq_len=seq_len),
        grid=(nt, nj),
        in_specs=[
            pl.BlockSpec((tm, d), lambda i, j: (i, 0)),
            pl.BlockSpec((8, d), lambda i, j: (i, 0)),
            pl.BlockSpec((None, d, tn), lambda i, j: (layer, 0, j)),
            pl.BlockSpec((None, d, tn), lambda i, j: (layer, 0, j + nj)),
            pl.BlockSpec((4, tn), lambda i, j: (0, j)),
            pl.BlockSpec((4, tn), lambda i, j: (0, j + nj)),
        ],
        out_specs=pl.BlockSpec((tm, tn), lambda i, j: (i, j)),
        out_shape=jax.ShapeDtypeStruct((m, d_ff), BF16),
        compiler_params=_cparams(("arbitrary", "arbitrary"), 48),
        name="ffn_up",
    )(h, halo.reshape(nt * 8, d), w_up, w_up, taps, taps)


def _dft_matrices(n):
    k = jnp.arange(n, dtype=jnp.int32)
    phase = ((k[:, None] * k[None, :]) % n).astype(F32) * (2.0 * np.pi / n)
    scale = 1.0 / np.sqrt(n)
    return (jnp.cos(phase) * scale).astype(BF16), (jnp.sin(phase) * scale).astype(BF16)


def _fourier_kernel(u_ref, cl_ref, sl_ref, cc_ref, sc_ref, w_ref, o_ref):
    u = u_ref[...]
    zc = jnp.dot(u, cc_ref[...], preferred_element_type=F32).astype(BF16)
    zs = jnp.dot(u, sc_ref[...], preferred_element_type=F32).astype(BF16)
    f = (jnp.dot(cl_ref[...], zc, preferred_element_type=F32)
         - jnp.dot(sl_ref[...], zs, preferred_element_type=F32))
    o_ref[...] = jnp.dot(f.astype(BF16), w_ref[...], preferred_element_type=F32).astype(o_ref.dtype)


def fourier_mix(u, w_four, seq_len):
    m = u.shape[0]
    groups, gw, _ = w_four.shape
    cl, sl = _dft_matrices(seq_len)
    cc, sc = _dft_matrices(gw)
    const = lambda r, c: pl.BlockSpec((r, c), lambda b, g: (0, 0), pipeline_mode=pl.Buffered(1))
    return pl.pallas_call(
        _fourier_kernel,
        grid=(m // seq_len, groups),
        in_specs=[
            pl.BlockSpec((seq_len, gw), lambda b, g: (b, g)),
            const(seq_len, seq_len), const(seq_len, seq_len), const(gw, gw), const(gw, gw),
            pl.BlockSpec((None, gw, gw), lambda b, g: (g, 0, 0)),
        ],
        out_specs=pl.BlockSpec((seq_len, gw), lambda b, g: (b, g)),
        out_shape=jax.ShapeDtypeStruct((m, groups * gw), BF16),
        compiler_params=_cparams(("arbitrary", "arbitrary"), 56),
        name="fourier_mix",
    )(u, cl, sl, cc, sc, w_four)


def _pool_matrices(seq_len):
    t = jnp.arange(seq_len, dtype=jnp.int32)
    bands, inv = [], []
    for w in POOL_WINDOWS:
        lo = jnp.clip(t - w // 2, 0, seq_len)
        hi = jnp.clip(t - w // 2 + w, 0, seq_len)
        bands.append(((t[None, :] >= lo[:, None]) & (t[None, :] < hi[:, None])).astype(BF16))
        inv.append(1.0 / (hi - lo).astype(F32))
    return jnp.stack(bands), jnp.stack(inv)[:, :, None]


def _pool_kernel(u_ref, band_ref, inv_ref, w_ref, ps_ref, o_ref):
    seq_len = u_ref.shape[0]
    rows = min(POOL_ROWS, seq_len)
    span = min(2 * rows, seq_len)
    assert seq_len % rows == 0 and max(POOL_WINDOWS) <= rows // 2
    for r0 in range(0, seq_len, rows):
        c0 = min(max(r0 - rows // 2, 0), seq_len - span)
        u = u_ref[c0:c0 + span, :]
        sums = jnp.dot(band_ref[r0:r0 + rows, c0:c0 + span], u, preferred_element_type=F32)
        pooled = sums * inv_ref[r0:r0 + rows, :] - u_ref[r0:r0 + rows, :].astype(F32)
        y = jnp.dot(pooled.astype(BF16), w_ref[...], preferred_element_type=F32)
        o_ref[r0:r0 + rows, :] = (y * ps_ref[...]).astype(o_ref.dtype)


def pool_mix(u, w_pool, pool_scale, seq_len, group_off):
    m = u.shape[0]
    groups, gw, _ = w_pool.shape
    band, inv = _pool_matrices(seq_len)
    return pl.pallas_call(
        _pool_kernel,
        grid=(groups, m // seq_len),
        in_specs=[
            pl.BlockSpec((seq_len, gw), lambda g, b: (b, g + group_off)),
            pl.BlockSpec((None, seq_len, seq_len), lambda g, b: (g, 0, 0)),
            pl.BlockSpec((None, seq_len, 1), lambda g, b: (g, 0, 0)),
            pl.BlockSpec((None, gw, gw), lambda g, b: (g, 0, 0)),
            pl.BlockSpec((1, gw), lambda g, b: (0, g)),
        ],
        out_specs=pl.BlockSpec((seq_len, gw), lambda g, b: (b, g)),
        out_shape=jax.ShapeDtypeStruct((m, groups * gw), BF16),
        compiler_params=_cparams(("arbitrary", "arbitrary"), 48),
        name="pool_mix",
    )(u, band, inv, w_pool, pool_scale.reshape(1, groups * gw))


_NT_DIMS = (((1,), (1,)), ((), ()))


def _ctx_attn_kernel(q_ref, k_ref, v_ref, o_ref):
    n_heads, _, dh = q_ref.shape
    for h in range(n_heads):
        k = k_ref[h].astype(BF16)
        v = v_ref[h].astype(BF16)
        s = lax.dot_general(q_ref[h], k, _NT_DIMS, preferred_element_type=F32)
        p = jnp.exp(s - jnp.max(s, axis=-1, keepdims=True))
        denom = jnp.sum(p, axis=-1, keepdims=True)
        o = jnp.dot(p.astype(BF16), v, preferred_element_type=F32) / denom
        o_ref[:, h * dh:(h + 1) * dh] = o.astype(o_ref.dtype)


def context_attention(q, k, v):
    b, n_heads, seq_len, dh = q.shape
    blk = pl.BlockSpec((None, n_heads, seq_len, dh), lambda i: (i, 0, 0, 0))
    return pl.pallas_call(
        _ctx_attn_kernel,
        grid=(b,),
        in_specs=[blk, blk, blk],
        out_specs=pl.BlockSpec((seq_len, n_heads * dh), lambda i: (i, 0)),
        out_shape=jax.ShapeDtypeStruct((b * seq_len, n_heads * dh), BF16),
        compiler_params=_cparams(("arbitrary",), 48),
        name="context_attention",
    )(q, k, v)


def _na_structure(rows):
    kr = min(NA_ROWS, rows)
    assert rows % NA_TILE_ROWS == 0 and rows >= NA_WIN_ROWS
    n_tiles = rows // NA_TILE_ROWS
    starts = np.clip(np.arange(n_tiles) * NA_TILE_ROWS - kr // 2, 0, rows - NA_WIN_ROWS)
    patterns, pattern_of = [], []
    for t in range(n_tiles):
        r = t * NA_TILE_ROWS + np.arange(NA_TILE_ROWS)[:, None]
        ka = starts[t] + np.arange(NA_WIN_ROWS)[None, :]
        rs = np.clip(r - kr // 2, 0, rows - kr)
        assert (rs >= starts[t]).all() and (rs + kr <= starts[t] + NA_WIN_ROWS).all()
        slot = np.where((ka >= rs) & (ka < rs + kr), ka - r + NA_ROWS - 1, 2 * NA_ROWS - 1)
        for p, known in enumerate(patterns):
            if (known == slot).all():
                pattern_of.append(p)
                break
        else:
            pattern_of.append(len(patterns))
            patterns.append(slot)
    return tuple(int(s) for s in starts), tuple(pattern_of), np.stack(patterns)


def _na_bias(rpb, dr_slot):
    n_heads = rpb.shape[0]
    qc = np.arange(GRID_W)[:, None]
    kc = np.arange(GRID_W)[None, :]
    col_start = np.clip(qc - NA_COLS // 2, 0, GRID_W - NA_COLS)
    col_ok = (kc >= col_start) & (kc < col_start + NA_COLS)
    dc = np.clip(kc - qc + NA_COLS - 1, 0, 2 * NA_COLS - 2)
    onehot = (dc.reshape(1, -1) == np.arange(2 * NA_COLS - 1)[:, None]).astype(np.float32)
    toep = jnp.einsum("hds,sn->hdn", rpb, jnp.asarray(onehot), precision=lax.Precision.HIGHEST)
    toep = jnp.where(jnp.asarray(col_ok)[None, None], toep.reshape(n_heads, 2 * NA_ROWS - 1, GRID_W, GRID_W), NEG)
    toep = jnp.concatenate([toep, jnp.full((n_heads, 1, GRID_W, GRID_W), NEG, F32)], axis=1)
    n_pat = dr_slot.shape[0]
    strips = [jnp.concatenate([toep[:, int(s)] for s in dr_slot[p, qr]], axis=-1)
              for p in range(n_pat) for qr in range(NA_TILE_ROWS)]
    return jnp.stack(strips, axis=1).reshape(n_heads, n_pat, NA_TILE_ROWS * GRID_W, NA_WIN_ROWS * GRID_W)


def _na_kernel(q_ref, k_ref, v_ref, ck_ref, cv_ref, bias_ref, o_ref, *, starts, pattern_of):
    tq = NA_TILE_ROWS * GRID_W
    tk = NA_WIN_ROWS * GRID_W
    ck = ck_ref[...].astype(BF16)
    cv = cv_ref[...].astype(BF16)

    def scores(t):
        q = q_ref[t * tq:(t + 1) * tq, :]
        kw = k_ref[starts[t] * GRID_W:starts[t] * GRID_W + tk, :]
        s_loc = lax.dot_general(q, kw, _NT_DIMS, preferred_element_type=F32) + bias_ref[pattern_of[t]]
        s_ctx = lax.dot_general(q, ck, _NT_DIMS, preferred_element_type=F32)
        return s_loc, s_ctx

    def finish(t, s_loc, s_ctx):
        vw = v_ref[starts[t] * GRID_W:starts[t] * GRID_W + tk, :]
        mx = jnp.maximum(jnp.max(s_loc, axis=-1, keepdims=True), jnp.max(s_ctx, axis=-1, keepdims=True))
        p_loc = jnp.exp(s_loc - mx)
        p_ctx = jnp.exp(s_ctx - mx)
        denom = jnp.sum(p_loc, axis=-1, keepdims=True) + jnp.sum(p_ctx, axis=-1, keepdims=True)
        o = (jnp.dot(p_loc.astype(BF16), vw, preferred_element_type=F32)
             + jnp.dot(p_ctx.astype(BF16), cv, preferred_element_type=F32)) / denom
        o_ref[t * tq:(t + 1) * tq, :] = o.astype(o_ref.dtype)

    n_tiles = len(starts)
    sc = scores(0)
    for t in range(n_tiles):
        nxt = scores(t + 1) if t + 1 < n_tiles else None
        finish(t, *sc)
        sc = nxt


def neighbourhood_attention(q, k, v, ctx_k, ctx_v, layer, rpb):
    b, n_heads, seq_len, dh = q.shape
    lc = ctx_k.shape[3]
    starts, pattern_of, dr_slot = _na_structure(seq_len // GRID_W)
    bias = _na_bias(rpb, dr_slot)
    _, n_pat, tq, tk = bias.shape
    qkv = pl.BlockSpec((None, None, seq_len, dh), lambda h, i: (i, h, 0, 0))
    ctx = pl.BlockSpec((None, None, None, lc, dh), lambda h, i: (i, layer, h, 0, 0))
    return pl.pallas_call(
        functools.partial(_na_kernel, starts=starts, pattern_of=pattern_of),
        grid=(n_heads, b),
        in_specs=[qkv, qkv, qkv, ctx, ctx,
                  pl.BlockSpec((None, n_pat, tq, tk), lambda h, i: (h, 0, 0, 0))],
        out_specs=pl.BlockSpec((seq_len, dh), lambda h, i: (i, h)),
        out_shape=jax.ShapeDtypeStruct((b * seq_len, n_heads * dh), BF16),
        compiler_params=_cparams(("arbitrary", "arbitrary"), 48),
        name="neighbourhood_attention",
    )(q, k, v, ctx_k, ctx_v, bias)


SHIFT1, SCALE1, GATE1, SHIFT2, SCALE2, GATE2 = range(6)


def _trunk(x, seq_len, mod, row_of, ctx_kv, wts, alpha):
    (ln1_g, ln1_b, ln2_g, ln2_b, w_in, w_four, w_pool, pool_scale, w_out_a, w_qkv, rpb, w_out_c,
     w_up, conv_w, conv_b, w_down) = wts
    depth = mod.shape[0]
    d = x.shape[1]
    new_k, new_v = [], []
    h = ln_modulate(x, mod[0], row_of, SCALE1, SHIFT1)
    for i in range(depth):
        j = i // 2
        if i % 2 == 0:
            u = matmul([h], w_in, j, n=d)
            ya = fourier_mix(u, w_four[j], seq_len)
            yb = pool_mix(u, w_pool[j], pool_scale[j], seq_len, w_four.shape[1])
            y = matmul([ya, yb], w_out_a, j, n=d)
        else:
            q = matmul([h], w_qkv, j, n=d, col_off=0, scale=HEAD_DIM ** -0.5, seq_len=seq_len)
            if ctx_kv is None:
                k = matmul([h], w_qkv, j, n=d, col_off=d, out_dtype=F32, seq_len=seq_len)
                v = matmul([h], w_qkv, j, n=d, col_off=2 * d, out_dtype=F32, seq_len=seq_len)
                new_k.append(k)
                new_v.append(v)
                o = context_attention(q, k, v)
            else:
                k = matmul([h], w_qkv, j, n=d, col_off=d, seq_len=seq_len)
                v = matmul([h], w_qkv, j, n=d, col_off=2 * d, seq_len=seq_len)
                o = neighbourhood_attention(q, k, v, ctx_kv[0], ctx_kv[1], j, rpb[j])
            y = matmul([o], w_out_c, j, n=d)
        x, h = residual_ln(x, y, mod[i], row_of, GATE1, ln1_g[i], ln1_b[i], alpha, nxt=(mod[i], SCALE2, SHIFT2))
        f = matmul_ksplit(ffn_up(h, w_up, conv_w, conv_b, i, seq_len), w_down, i)
        nxt = (mod[i + 1], SCALE1, SHIFT1) if i + 1 < depth else None
        x, h = residual_ln(x, f, mod[i], row_of, GATE2, ln2_g[i], ln2_b[i], alpha, nxt=nxt)
    return x, new_k, new_v


def kernel(x_prompt, x_sample, cache_k, cache_v, c, c_ctx, w_ada, b_ada, ln1_g, ln1_b, ln2_g, ln2_b,
           w_in, w_four, w_pool, pool_scale, w_out_a, w_qkv, rpb, w_out_c, w_up, conv_w, conv_b, w_down):
    batch, seq, d = x_prompt.shape
    dec_batch, dec_seq, _ = x_sample.shape
    depth = w_ada.shape[0]
    alpha = float((2 * depth) ** 0.25)
    assert dec_batch < COND_ROWS

    cond = jnp.concatenate([c, c_ctx[None, :], jnp.zeros((COND_ROWS - dec_batch - 1, d), F32)], axis=0)
    mod = ada_modulation(cond, w_ada, b_ada).reshape(depth, COND_ROWS, 1, 6 * d)

    wts = (ln1_g, ln1_b, ln2_g, ln2_b,
           w_in.astype(BF16), w_four.astype(BF16), w_pool.astype(BF16), pool_scale, w_out_a.astype(BF16),
           w_qkv.astype(BF16), rpb, w_out_c.astype(BF16),
           w_up.astype(BF16), conv_w, conv_b, w_down.astype(BF16))

    y_prompt, ks, vs = _trunk(x_prompt.reshape(batch * seq, d), seq, mod, lambda r: dec_batch, None, wts, alpha)
    y_sample, _, _ = _trunk(x_sample.reshape(dec_batch * dec_seq, d), dec_seq, mod, lambda r: r // dec_seq,
                            (cache_k, cache_v), wts, alpha)
    return (y_prompt.reshape(batch, seq, d), y_sample.reshape(dec_batch, dec_seq, d),
            jnp.stack(ks, axis=1), jnp.stack(vs, axis=1))
```

```python
import functools

import numpy as np
import jax
import jax.numpy as jnp
from jax import lax
from jax.experimental import pallas as pl
from jax.experimental.pallas import tpu as pltpu

F32 = jnp.float32
BF16 = jnp.bfloat16

LN_EPS = 1e-5
NEG = -1e30
GRID_W = 64
NA_ROWS = 8
NA_COLS = 16
HEAD_DIM = 128
POOL_WINDOWS = (2, 4, 8, 16)
COND_ROWS = 16
NA_TILE_ROWS = 4
NA_WIN_ROWS = 12
LN_CHUNK = 64
POOL_ROWS = 256
SHORT_SEQ = 512
MIB = 1024 * 1024


def _cparams(semantics, vmem_mib):
    return pltpu.CompilerParams(dimension_semantics=semantics, vmem_limit_bytes=int(vmem_mib * MIB))


def _tile(dim, pref):
    t = min(dim, pref)
    assert dim % t == 0, (dim, pref)
    return t


def _ada_kernel(c_ref, w_ref, b_ref, o_ref):
    c = c_ref[...]
    s = (c * jax.nn.sigmoid(c)).astype(BF16)
    o_ref[...] = jnp.dot(s, w_ref[...].astype(BF16), preferred_element_type=F32) + b_ref[...]


def ada_modulation(cond, w_ada, b_ada):
    depth, d, n = w_ada.shape
    tn = _tile(n, 512)
    return pl.pallas_call(
        _ada_kernel,
        grid=(depth, n // tn),
        in_specs=[
            pl.BlockSpec((COND_ROWS, d), lambda l, j: (0, 0)),
            pl.BlockSpec((None, d, tn), lambda l, j: (l, 0, j)),
            pl.BlockSpec((None, 1, tn), lambda l, j: (l, 0, j)),
        ],
        out_specs=pl.BlockSpec((None, COND_ROWS, tn), lambda l, j: (l, 0, j)),
        out_shape=jax.ShapeDtypeStruct((depth, COND_ROWS, n), F32),
        compiler_params=_cparams(("arbitrary", "arbitrary"), 40),
        name="ada_modulation",
    )(cond, w_ada, b_ada.reshape(depth, 1, n))


def _normalise(x):
    mu = jnp.mean(x, axis=-1, keepdims=True)
    xc = x - mu
    var = jnp.mean(xc * xc, axis=-1, keepdims=True)
    return xc * lax.rsqrt(var + LN_EPS)


def _row_chunks(n_rows, body):
    chunk = min(LN_CHUNK, n_rows)
    assert n_rows % chunk == 0

    def step(c, carry):
        body(pl.ds(pl.multiple_of(c * chunk, chunk), chunk))
        return carry

    lax.fori_loop(0, n_rows // chunk, step, 0)


def _ln_mod_kernel(x_ref, sc_ref, sh_ref, h_ref):
    scale = 1.0 + sc_ref[...]
    shift = sh_ref[...]

    def body(rows):
        h_ref[rows, :] = (_normalise(x_ref[rows, :]) * scale + shift).astype(h_ref.dtype)

    _row_chunks(x_ref.shape[0], body)


def _mod_spec(d, row_of_tile, which):
    return pl.BlockSpec((None, 1, d), lambda i: (row_of_tile(i), 0, which))


def ln_modulate(x, mod, row_of, which_scale, which_shift):
    m, d = x.shape
    tm = _tile(m, 512)
    row = lambda i: row_of(i * tm)
    return pl.pallas_call(
        _ln_mod_kernel,
        grid=(m // tm,),
        in_specs=[pl.BlockSpec((tm, d), lambda i: (i, 0)),
                  _mod_spec(d, row, which_scale), _mod_spec(d, row, which_shift)],
        out_specs=pl.BlockSpec((tm, d), lambda i: (i, 0)),
        out_shape=jax.ShapeDtypeStruct((m, d), BF16),
        compiler_params=_cparams(("arbitrary",), 40),
        name="ln_modulate",
    )(x, mod, mod)


def _res_ln_kernel(*refs, alpha, with_next):
    if with_next:
        x_ref, y_ref, g_ref, gam_ref, bet_ref, sc_ref, sh_ref, xo_ref, h_ref = refs
        scale = 1.0 + sc_ref[...]
        shift = sh_ref[...]
    else:
        x_ref, y_ref, g_ref, gam_ref, bet_ref, xo_ref = refs
    gate, gamma, beta = g_ref[...], gam_ref[...], bet_ref[...]

    def body(rows):
        z = alpha * x_ref[rows, :] + gate * y_ref[rows, :].astype(F32)
        xn = _normalise(z) * gamma + beta
        xo_ref[rows, :] = xn
        if with_next:
            h_ref[rows, :] = (_normalise(xn) * scale + shift).astype(h_ref.dtype)

    _row_chunks(x_ref.shape[0], body)


def residual_ln(x, y, mod, row_of, which_gate, gamma, beta, alpha, nxt=None):
    m, d = x.shape
    tm = _tile(m, 256)
    row = lambda i: row_of(i * tm)
    tile = pl.BlockSpec((tm, d), lambda i: (i, 0))
    vec = pl.BlockSpec((1, d), lambda i: (0, 0))
    in_specs = [tile, tile, _mod_spec(d, row, which_gate), vec, vec]
    args = [x, y, mod, gamma.reshape(1, d), beta.reshape(1, d)]
    out_specs = [tile]
    out_shape = [jax.ShapeDtypeStruct((m, d), F32)]
    if nxt is not None:
        in_specs += [_mod_spec(d, row, nxt[1]), _mod_spec(d, row, nxt[2])]
        args += [nxt[0], nxt[0]]
        out_specs.append(tile)
        out_shape.append(jax.ShapeDtypeStruct((m, d), BF16))
    out = pl.pallas_call(
        functools.partial(_res_ln_kernel, alpha=alpha, with_next=nxt is not None),
        grid=(m // tm,),
        in_specs=in_specs,
        out_specs=out_specs,
        out_shape=out_shape,
        compiler_params=_cparams(("arbitrary",), 48),
        name="residual_ln",
    )(*args)
    return (out[0], out[1]) if nxt is not None else (out[0], None)


def _mm_kernel(*refs, n_parts, scale, head_major):
    a_refs, w_refs, o_ref = refs[:n_parts], refs[n_parts:2 * n_parts], refs[2 * n_parts]
    acc = None
    for a_ref, w_ref in zip(a_refs, w_refs):
        part = jnp.dot(a_ref[...], w_ref[...], preferred_element_type=F32)
        acc = part if acc is None else acc + part
    if scale is not None:
        acc = acc * scale
    if not head_major:
        o_ref[...] = acc.astype(o_ref.dtype)
        return
    bt, nh, lb, dh = o_ref.shape
    for b in range(bt):
        for h in range(nh):
            o_ref[b, h] = acc[b * lb:(b + 1) * lb, h * dh:(h + 1) * dh].astype(o_ref.dtype)


def matmul(a_parts, w, layer, *, n, col_off=0, out_dtype=BF16, scale=None, seq_len=None, tm=1024, tn=1024):
    m, kp = a_parts[0].shape
    assert all(a.shape == (m, kp) for a in a_parts) and w.shape[1] == kp * len(a_parts)
    tm, tn = _tile(m, tm), _tile(n, tn)
    assert col_off % tn == 0
    joff = col_off // tn
    in_specs = [pl.BlockSpec((tm, kp), lambda i, j: (i, 0)) for _ in a_parts]
    in_specs += [pl.BlockSpec((None, kp, tn), functools.partial(lambda i, j, p: (layer, p, j + joff), p=p))
                 for p in range(len(a_parts))]
    if seq_len is None:
        out_spec = pl.BlockSpec((tm, tn), lambda i, j: (i, j))
        out_shape = jax.ShapeDtypeStruct((m, n), out_dtype)
    else:
        nh = tn // HEAD_DIM
        out_shape = jax.ShapeDtypeStruct((m // seq_len, n // HEAD_DIM, seq_len, HEAD_DIM), out_dtype)
        if tm >= seq_len:
            out_spec = pl.BlockSpec((tm // seq_len, nh, seq_len, HEAD_DIM), lambda i, j: (i, j, 0, 0))
        else:
            per = seq_len // tm
            out_spec = pl.BlockSpec((1, nh, tm, HEAD_DIM), lambda i, j: (i // per, j, i % per, 0))
    return pl.pallas_call(
        functools.partial(_mm_kernel, n_parts=len(a_parts), scale=scale, head_major=seq_len is not None),
        grid=(m // tm, n // tn),
        in_specs=in_specs,
        out_specs=out_spec,
        out_shape=out_shape,
        compiler_params=_cparams(("arbitrary", "arbitrary"), 56),
        name="matmul",
    )(*a_parts, *([w] * len(a_parts)))


def _mm_ksplit_kernel(a_ref, w_ref, o_ref, acc_ref, *, k_steps):
    part = jnp.dot(a_ref[...], w_ref[...], preferred_element_type=F32)
    s = pl.program_id(2)

    @pl.when(s == 0)
    def _():
        acc_ref[...] = part

    @pl.when((s > 0) & (s < k_steps - 1))
    def _():
        acc_ref[...] += part

    @pl.when(s == k_steps - 1)
    def _():
        o_ref[...] = (acc_ref[...] + part).astype(o_ref.dtype)


def matmul_ksplit(a, w, layer, *, out_dtype=BF16, tm=1024, tn=512, k_steps=2):
    m, k = a.shape
    n = w.shape[2]
    tm, tn = _tile(m, tm), _tile(n, tn)
    tk = k // k_steps
    assert k_steps >= 2 and tk * k_steps == k and tk % 128 == 0
    return pl.pallas_call(
        functools.partial(_mm_ksplit_kernel, k_steps=k_steps),
        grid=(m // tm, n // tn, k_steps),
        in_specs=[pl.BlockSpec((tm, tk), lambda i, j, s: (i, s)),
                  pl.BlockSpec((None, tk, tn), lambda i, j, s: (layer, s, j))],
        out_specs=pl.BlockSpec((tm, tn), lambda i, j, s: (i, j)),
        out_shape=jax.ShapeDtypeStruct((m, n), out_dtype),
        scratch_shapes=[pltpu.VMEM((tm, tn), F32)],
        compiler_params=_cparams(("arbitrary", "arbitrary", "arbitrary"), 56),
        name="matmul_ksplit",
    )(a, w)


def _ffn_up_kernel(a_ref, halo_ref, wa_ref, wg_ref, cwa_ref, cwg_ref, o_ref, *, seq_len):
    a = a_ref[...]
    halo = halo_ref[...]
    tm, tn = o_ref.shape
    sub = lax.broadcasted_iota(jnp.int32, (8, tn), 0)
    inner = range(seq_len, tm, seq_len)

    def patch(x, group, row, value):
        lo, hi = group * 8, (group + 1) * 8
        parts = [x[:lo]] * (lo > 0) + [jnp.where(sub == row, value, x[lo:hi])] + [x[hi:]] * (hi < tm)
        return jnp.concatenate(parts, axis=0)

    def conv(w_ref, cw_ref):
        w = w_ref[...]
        u = jnp.dot(a, w, preferred_element_type=F32)
        uh = jnp.dot(halo, w, preferred_element_type=F32)
        prev = patch(pltpu.roll(u, 1, 0), 0, 0, uh[7:8])
        nxt = patch(pltpu.roll(u, tm - 1, 0), tm // 8 - 1, 7, uh[0:1])
        for r in inner:
            prev = patch(prev, r // 8, 0, 0.0)
            nxt = patch(nxt, r // 8 - 1, 7, 0.0)
        cw = cw_ref[...]
        return prev * cw[0:1] + u * cw[1:2] + nxt * cw[2:3] + cw[3:4]

    act = conv(wa_ref, cwa_ref)
    half = 0.5 * conv(wg_ref, cwg_ref)
    o_ref[...] = ((half + half * jnp.tanh(half)) * act).astype(o_ref.dtype)


def ffn_up(h, w_up, conv_w, conv_b, layer, seq_len, *, tm=1024, tn=256):
    m, d = h.shape
    d_ff = w_up.shape[2] // 2
    tm, tn = _tile(m, tm), _tile(d_ff, tn)
    assert (tm % seq_len == 0 or seq_len % tm == 0) and tm % 16 == 0
    nt, nj = m // tm, d_ff // tn
    tiles = h.reshape(nt, tm, d)
    zero = jnp.zeros((1, d), h.dtype)
    starts = np.arange(nt) * tm
    before = jnp.concatenate([zero, tiles[:-1, -1]], axis=0)
    before = jnp.where(jnp.asarray(starts % seq_len == 0)[:, None], 0, before)
    after = jnp.concatenate([tiles[1:, 0], zero], axis=0)
    after = jnp.where(jnp.asarray((starts + tm) % seq_len == 0)[:, None], 0, after)
    halo = jnp.concatenate([after[:, None], jnp.zeros((nt, 6, d), h.dtype), before[:, None]], axis=1)
    taps = jnp.concatenate([conv_w[layer], conv_b[layer][None]], axis=0)
    return pl.pallas_call(
        functools.partial(_ffn_up_kernel, seq_len=seq_len),
        grid=(nt, nj),
        in_specs=[
            pl.BlockSpec((tm, d), lambda i, j: (i, 0)),
            pl.BlockSpec((8, d), lambda i, j: (i, 0)),
            pl.BlockSpec((None, d, tn), lambda i, j: (layer, 0, j)),
            pl.BlockSpec((None, d, tn), lambda i, j: (layer, 0, j + nj)),
            pl.BlockSpec((4, tn), lambda i, j: (0, j)),
            pl.BlockSpec((4, tn), lambda i, j: (0, j + nj)),
        ],
        out_specs=pl.BlockSpec((tm, tn), lambda i, j: (i, j)),
        out_shape=jax.ShapeDtypeStruct((m, d_ff), BF16),
        compiler_params=_cparams(("arbitrary", "arbitrary"), 48),
        name="ffn_up",
    )(h, halo.reshape(nt * 8, d), w_up, w_up, taps, taps)


def _dft_matrices(n):
    k = jnp.arange(n, dtype=jnp.int32)
    phase = ((k[:, None] * k[None, :]) % n).astype(F32) * (2.0 * np.pi / n)
    scale = 1.0 / np.sqrt(n)
    return (jnp.cos(phase) * scale).astype(BF16), (jnp.sin(phase) * scale).astype(BF16)


def _fourier_kernel(u_ref, ch_ref, sh_ref, cmid_ref, flip_ref, cc_ref, sc_ref, w_ref, o_ref):
    n = u_ref.shape[0]
    hn = n // 2
    gw = cc_ref.shape[0]
    sub = lax.broadcasted_iota(jnp.int32, (8, gw), 0)
    for g in range(w_ref.shape[0]):
        cols = slice(g * gw, (g + 1) * gw)
        u = u_ref[:, cols]
        zc = jnp.dot(u, cc_ref[...], preferred_element_type=F32).astype(BF16)
        zs = jnp.dot(u, sc_ref[...], preferred_element_type=F32).astype(BF16)
        p1 = jnp.dot(ch_ref[...], zc, preferred_element_type=F32)
        p2 = jnp.dot(sh_ref[...], zs, preferred_element_type=F32)
        mid = jnp.dot(cmid_ref[...], zc, preferred_element_type=F32)
        w = w_ref[g]
        o_ref[0:hn, cols] = jnp.dot((p1 - p2).astype(BF16), w, preferred_element_type=F32).astype(o_ref.dtype)
        back = jnp.dot(flip_ref[...], (p1 + p2).astype(BF16), preferred_element_type=F32)
        back = jnp.concatenate([jnp.where(sub == 0, mid[0:1], back[0:8]), back[8:]], axis=0)
        o_ref[hn:n, cols] = jnp.dot(back.astype(BF16), w, preferred_element_type=F32).astype(o_ref.dtype)


def fourier_mix(u, w_four, seq_len):
    m = u.shape[0]
    groups, gw, _ = w_four.shape
    gps = groups if seq_len <= SHORT_SEQ else 1
    hn = seq_len // 2
    assert hn % 8 == 0
    cl, sl = _dft_matrices(seq_len)
    cc, sc = _dft_matrices(gw)
    r = jnp.arange(hn, dtype=jnp.int32)
    flip = ((r[:, None] + r[None, :] == hn) & (r[:, None] >= 1)).astype(BF16)
    const = lambda shape, idx: pl.BlockSpec(shape, lambda b, g: idx, pipeline_mode=pl.Buffered(1))
    return pl.pallas_call(
        _fourier_kernel,
        grid=(m // seq_len, groups // gps),
        in_specs=[
            pl.BlockSpec((seq_len, gw * gps), lambda b, g: (b, g)),
            const((hn, seq_len), (0, 0)), const((hn, seq_len), (0, 0)), const((8, seq_len), (hn // 8, 0)),
            const((hn, hn), (0, 0)), const((gw, gw), (0, 0)), const((gw, gw), (0, 0)),
            pl.BlockSpec((gps, gw, gw), lambda b, g: (g, 0, 0)),
        ],
        out_specs=pl.BlockSpec((seq_len, gw * gps), lambda b, g: (b, g)),
        out_shape=jax.ShapeDtypeStruct((m, groups * gw), BF16),
        compiler_params=_cparams(("arbitrary", "arbitrary"), 56),
        name="fourier_mix",
    )(u, cl, sl, cl, flip, cc, sc, w_four)


def _pool_matrices(seq_len):
    t = jnp.arange(seq_len, dtype=jnp.int32)
    bands, inv = [], []
    for w in POOL_WINDOWS:
        lo = jnp.clip(t - w // 2, 0, seq_len)
        hi = jnp.clip(t - w // 2 + w, 0, seq_len)
        bands.append(((t[None, :] >= lo[:, None]) & (t[None, :] < hi[:, None])).astype(BF16))
        inv.append(1.0 / (hi - lo).astype(F32))
    return jnp.stack(bands), jnp.stack(inv)[:, :, None]


def _pool_kernel(u_ref, band_ref, inv_ref, w_ref, ps_ref, o_ref):
    seq_len = u_ref.shape[0]
    rows = min(POOL_ROWS, seq_len)
    span = min(2 * rows, seq_len)
    assert seq_len % rows == 0 and max(POOL_WINDOWS) <= rows // 2
    gw = w_ref.shape[-1]
    for g in range(w_ref.shape[0]):
        cols = slice(g * gw, (g + 1) * gw)
        for r0 in range(0, seq_len, rows):
            c0 = min(max(r0 - rows // 2, 0), seq_len - span)
            u = u_ref[c0:c0 + span, cols]
            sums = jnp.dot(band_ref[g, r0:r0 + rows, c0:c0 + span], u, preferred_element_type=F32)
            pooled = sums * inv_ref[g, r0:r0 + rows, :] - u_ref[r0:r0 + rows, cols].astype(F32)
            y = jnp.dot(pooled.astype(BF16), w_ref[g], preferred_element_type=F32)
            o_ref[r0:r0 + rows, cols] = (y * ps_ref[:, cols]).astype(o_ref.dtype)


def pool_mix(u, w_pool, pool_scale, seq_len, group_off):
    m = u.shape[0]
    groups, gw, _ = w_pool.shape
    gps = groups if seq_len <= SHORT_SEQ else 1
    assert group_off % gps == 0
    band, inv = _pool_matrices(seq_len)
    return pl.pallas_call(
        _pool_kernel,
        grid=(groups // gps, m // seq_len),
        in_specs=[
            pl.BlockSpec((seq_len, gw * gps), lambda g, b: (b, g + group_off // gps)),
            pl.BlockSpec((gps, seq_len, seq_len), lambda g, b: (g, 0, 0)),
            pl.BlockSpec((gps, seq_len, 1), lambda g, b: (g, 0, 0)),
            pl.BlockSpec((gps, gw, gw), lambda g, b: (g, 0, 0)),
            pl.BlockSpec((1, gw * gps), lambda g, b: (0, g)),
        ],
        out_specs=pl.BlockSpec((seq_len, gw * gps), lambda g, b: (b, g)),
        out_shape=jax.ShapeDtypeStruct((m, groups * gw), BF16),
        compiler_params=_cparams(("arbitrary", "arbitrary"), 48),
        name="pool_mix",
    )(u, band, inv, w_pool, pool_scale.reshape(1, groups * gw))


_NT_DIMS = (((1,), (1,)), ((), ()))


def _ctx_attn_kernel(q_ref, k_ref, v_ref, o_ref):
    n_heads, _, dh = q_ref.shape
    for h in range(n_heads):
        k = k_ref[h].astype(BF16)
        v = v_ref[h].astype(BF16)
        s = lax.dot_general(q_ref[h], k, _NT_DIMS, preferred_element_type=F32)
        p = jnp.exp(s - jnp.max(s, axis=-1, keepdims=True))
        denom = jnp.sum(p, axis=-1, keepdims=True)
        o = jnp.dot(p.astype(BF16), v, preferred_element_type=F32) / denom
        o_ref[:, h * dh:(h + 1) * dh] = o.astype(o_ref.dtype)


def context_attention(q, k, v):
    b, n_heads, seq_len, dh = q.shape
    blk = pl.BlockSpec((None, n_heads, seq_len, dh), lambda i: (i, 0, 0, 0))
    return pl.pallas_call(
        _ctx_attn_kernel,
        grid=(b,),
        in_specs=[blk, blk, blk],
        out_specs=pl.BlockSpec((seq_len, n_heads * dh), lambda i: (i, 0)),
        out_shape=jax.ShapeDtypeStruct((b * seq_len, n_heads * dh), BF16),
        compiler_params=_cparams(("arbitrary",), 48),
        name="context_attention",
    )(q, k, v)


def _na_structure(rows):
    kr = min(NA_ROWS, rows)
    assert rows % NA_TILE_ROWS == 0 and rows >= NA_WIN_ROWS
    n_tiles = rows // NA_TILE_ROWS
    starts = np.clip(np.arange(n_tiles) * NA_TILE_ROWS - kr // 2, 0, rows - NA_WIN_ROWS)
    patterns, pattern_of = [], []
    for t in range(n_tiles):
        r = t * NA_TILE_ROWS + np.arange(NA_TILE_ROWS)[:, None]
        ka = starts[t] + np.arange(NA_WIN_ROWS)[None, :]
        rs = np.clip(r - kr // 2, 0, rows - kr)
        assert (rs >= starts[t]).all() and (rs + kr <= starts[t] + NA_WIN_ROWS).all()
        slot = np.where((ka >= rs) & (ka < rs + kr), ka - r + NA_ROWS - 1, 2 * NA_ROWS - 1)
        for p, known in enumerate(patterns):
            if (known == slot).all():
                pattern_of.append(p)
                break
        else:
            pattern_of.append(len(patterns))
            patterns.append(slot)
    return tuple(int(s) for s in starts), tuple(pattern_of), np.stack(patterns)


def _na_bias(rpb, dr_slot):
    n_heads = rpb.shape[0]
    qc = np.arange(GRID_W)[:, None]
    kc = np.arange(GRID_W)[None, :]
    col_start = np.clip(qc - NA_COLS // 2, 0, GRID_W - NA_COLS)
    col_ok = (kc >= col_start) & (kc < col_start + NA_COLS)
    dc = np.clip(kc - qc + NA_COLS - 1, 0, 2 * NA_COLS - 2)
    onehot = (dc.reshape(1, -1) == np.arange(2 * NA_COLS - 1)[:, None]).astype(np.float32)
    toep = jnp.einsum("hds,sn->hdn", rpb, jnp.asarray(onehot), precision=lax.Precision.HIGHEST)
    toep = jnp.where(jnp.asarray(col_ok)[None, None], toep.reshape(n_heads, 2 * NA_ROWS - 1, GRID_W, GRID_W), NEG)
    toep = jnp.concatenate([toep, jnp.full((n_heads, 1, GRID_W, GRID_W), NEG, F32)], axis=1)
    n_pat = dr_slot.shape[0]
    strips = [jnp.concatenate([toep[:, int(s)] for s in dr_slot[p, qr]], axis=-1)
              for p in range(n_pat) for qr in range(NA_TILE_ROWS)]
    return jnp.stack(strips, axis=1).reshape(n_heads, n_pat, NA_TILE_ROWS * GRID_W, NA_WIN_ROWS * GRID_W)


def _na_kernel(q_ref, k_ref, v_ref, ck_ref, cv_ref, bias_ref, o_ref, *, starts, pattern_of):
    tq = NA_TILE_ROWS * GRID_W
    tk = NA_WIN_ROWS * GRID_W
    ck = ck_ref[...].astype(BF16)
    cv = cv_ref[...].astype(BF16)

    def scores(t):
        q = q_ref[t * tq:(t + 1) * tq, :]
        kw = k_ref[starts[t] * GRID_W:starts[t] * GRID_W + tk, :]
        s_loc = lax.dot_general(q, kw, _NT_DIMS, preferred_element_type=F32) + bias_ref[pattern_of[t]]
        s_ctx = lax.dot_general(q, ck, _NT_DIMS, preferred_element_type=F32)
        return s_loc, s_ctx

    def finish(t, s_loc, s_ctx):
        vw = v_ref[starts[t] * GRID_W:starts[t] * GRID_W + tk, :]
        mx = jnp.maximum(jnp.max(s_loc, axis=-1, keepdims=True), jnp.max(s_ctx, axis=-1, keepdims=True))
        p_loc = jnp.exp(s_loc - mx)
        p_ctx = jnp.exp(s_ctx - mx)
        denom = jnp.sum(p_loc, axis=-1, keepdims=True) + jnp.sum(p_ctx, axis=-1, keepdims=True)
        o = (jnp.dot(p_loc.astype(BF16), vw, preferred_element_type=F32)
             + jnp.dot(p_ctx.astype(BF16), cv, preferred_element_type=F32)) / denom
        o_ref[t * tq:(t + 1) * tq, :] = o.astype(o_ref.dtype)

    n_tiles = len(starts)
    sc = scores(0)
    for t in range(n_tiles):
        nxt = scores(t + 1) if t + 1 < n_tiles else None
        finish(t, *sc)
        sc = nxt


def neighbourhood_attention(q, k, v, ctx_k, ctx_v, layer, rpb):
    b, n_heads, seq_len, dh = q.shape
    lc = ctx_k.shape[3]
    starts, pattern_of, dr_slot = _na_structure(seq_len // GRID_W)
    bias = _na_bias(rpb, dr_slot)
    _, n_pat, tq, tk = bias.shape
    qkv = pl.BlockSpec((None, None, seq_len, dh), lambda h, i: (i, h, 0, 0))
    ctx = pl.BlockSpec((None, None, None, lc, dh), lambda h, i: (i, layer, h, 0, 0))
    return pl.pallas_call(
        functools.partial(_na_kernel, starts=starts, pattern_of=pattern_of),
        grid=(n_heads, b),
        in_specs=[qkv, qkv, qkv, ctx, ctx,
                  pl.BlockSpec((None, n_pat, tq, tk), lambda h, i: (h, 0, 0, 0))],
        out_specs=pl.BlockSpec((seq_len, dh), lambda h, i: (i, h)),
        out_shape=jax.ShapeDtypeStruct((b * seq_len, n_heads * dh), BF16),
        compiler_params=_cparams(("arbitrary", "arbitrary"), 48),
        name="neighbourhood_attention",
    )(q, k, v, ctx_k, ctx_v, bias)


SHIFT1, SCALE1, GATE1, SHIFT2, SCALE2, GATE2 = range(6)


def _trunk(x, seq_len, mod, row_of, ctx_kv, wts, alpha):
    (ln1_g, ln1_b, ln2_g, ln2_b, w_in, w_four, w_pool, pool_scale, w_out_a, w_qkv, rpb, w_out_c,
     w_up, conv_w, conv_b, w_down) = wts
    depth = mod.shape[0]
    d = x.shape[1]
    new_k, new_v = [], []
    h = ln_modulate(x, mod[0], row_of, SCALE1, SHIFT1)
    for i in range(depth):
        j = i // 2
        if i % 2 == 0:
            u = matmul([h], w_in, j, n=d)
            ya = fourier_mix(u, w_four[j], seq_len)
            yb = pool_mix(u, w_pool[j], pool_scale[j], seq_len, w_four.shape[1])
            y = matmul([ya, yb], w_out_a, j, n=d)
        else:
            q = matmul([h], w_qkv, j, n=d, col_off=0, scale=HEAD_DIM ** -0.5, seq_len=seq_len)
            if ctx_kv is None:
                k = matmul([h], w_qkv, j, n=d, col_off=d, out_dtype=F32, seq_len=seq_len)
                v = matmul([h], w_qkv, j, n=d, col_off=2 * d, out_dtype=F32, seq_len=seq_len)
                new_k.append(k)
                new_v.append(v)
                o = context_attention(q, k, v)
            else:
                k = matmul([h], w_qkv, j, n=d, col_off=d, seq_len=seq_len)
                v = matmul([h], w_qkv, j, n=d, col_off=2 * d, seq_len=seq_len)
                o = neighbourhood_attention(q, k, v, ctx_kv[0], ctx_kv[1], j, rpb[j])
            y = matmul([o], w_out_c, j, n=d)
        x, h = residual_ln(x, y, mod[i], row_of, GATE1, ln1_g[i], ln1_b[i], alpha, nxt=(mod[i], SCALE2, SHIFT2))
        f = matmul_ksplit(ffn_up(h, w_up, conv_w, conv_b, i, seq_len), w_down, i)
        nxt = (mod[i + 1], SCALE1, SHIFT1) if i + 1 < depth else None
        x, h = residual_ln(x, f, mod[i], row_of, GATE2, ln2_g[i], ln2_b[i], alpha, nxt=nxt)
    return x, new_k, new_v


def kernel(x_prompt, x_sample, cache_k, cache_v, c, c_ctx, w_ada, b_ada, ln1_g, ln1_b, ln2_g, ln2_b,
           w_in, w_four, w_pool, pool_scale, w_out_a, w_qkv, rpb, w_out_c, w_up, conv_w, conv_b, w_down):
    batch, seq, d = x_prompt.shape
    dec_batch, dec_seq, _ = x_sample.shape
    depth = w_ada.shape[0]
    alpha = float((2 * depth) ** 0.25)
    assert dec_batch < COND_ROWS

    cond = jnp.concatenate([c, c_ctx[None, :], jnp.zeros((COND_ROWS - dec_batch - 1, d), F32)], axis=0)
    mod = ada_modulation(cond, w_ada, b_ada).reshape(depth, COND_ROWS, 1, 6 * d)

    wts = (ln1_g, ln1_b, ln2_g, ln2_b,
           w_in.astype(BF16), w_four.astype(BF16), w_pool.astype(BF16), pool_scale, w_out_a.astype(BF16),
           w_qkv.astype(BF16), rpb, w_out_c.astype(BF16),
           w_up.astype(BF16), conv_w, conv_b, w_down.astype(BF16))

    y_prompt, ks, vs = _trunk(x_prompt.reshape(batch * seq, d), seq, mod, lambda r: dec_batch, None, wts, alpha)
    y_sample, _, _ = _trunk(x_sample.reshape(dec_batch * dec_seq, d), dec_seq, mod, lambda r: r // dec_seq,
                            (cache_k, cache_v), wts, alpha)
    return (y_prompt.reshape(batch, seq, d), y_sample.reshape(dec_batch, dec_seq, d),
            jnp.stack(ks, axis=1), jnp.stack(vs, axis=1))
```

```python
import functools

import numpy as np
import jax
import jax.numpy as jnp
from jax import lax
from jax.experimental import pallas as pl
from jax.experimental.pallas import tpu as pltpu

F32 = jnp.float32
BF16 = jnp.bfloat16

LN_EPS = 1e-5
NEG = -1e30
GRID_W = 64
NA_ROWS = 8
NA_COLS = 16
HEAD_DIM = 128
POOL_WINDOWS = (2, 4, 8, 16)
COND_ROWS = 16
NA_TILE_ROWS = 4
NA_WIN_ROWS = 12
LN_CHUNK = 64
POOL_ROWS = 256
SHORT_SEQ = 512
MIB = 1024 * 1024


def _cparams(semantics, vmem_mib):
    return pltpu.CompilerParams(dimension_semantics=semantics, vmem_limit_bytes=int(vmem_mib * MIB))


def _tile(dim, pref):
    t = min(dim, pref)
    assert dim % t == 0, (dim, pref)
    return t


def _ada_kernel(c_ref, w_ref, b_ref, o_ref):
    c = c_ref[...]
    s = (c * jax.nn.sigmoid(c)).astype(BF16)
    o_ref[...] = jnp.dot(s, w_ref[...].astype(BF16), preferred_element_type=F32) + b_ref[...]


def ada_modulation(cond, w_ada, b_ada):
    depth, d, n = w_ada.shape
    tn = _tile(n, 512)
    return pl.pallas_call(
        _ada_kernel,
        grid=(depth, n // tn),
        in_specs=[
            pl.BlockSpec((COND_ROWS, d), lambda l, j: (0, 0)),
            pl.BlockSpec((None, d, tn), lambda l, j: (l, 0, j)),
            pl.BlockSpec((None, 1, tn), lambda l, j: (l, 0, j)),
        ],
        out_specs=pl.BlockSpec((None, COND_ROWS, tn), lambda l, j: (l, 0, j)),
        out_shape=jax.ShapeDtypeStruct((depth, COND_ROWS, n), F32),
        compiler_params=_cparams(("arbitrary", "arbitrary"), 40),
        name="ada_modulation",
    )(cond, w_ada, b_ada.reshape(depth, 1, n))


def _normalise(x):
    mu = jnp.mean(x, axis=-1, keepdims=True)
    xc = x - mu
    var = jnp.mean(xc * xc, axis=-1, keepdims=True)
    return xc * lax.rsqrt(var + LN_EPS)


def _row_chunks(n_rows, body):
    chunk = min(LN_CHUNK, n_rows)
    assert n_rows % chunk == 0

    def step(c, carry):
        body(pl.ds(pl.multiple_of(c * chunk, chunk), chunk))
        return carry

    lax.fori_loop(0, n_rows // chunk, step, 0)


def _ln_mod_kernel(x_ref, sc_ref, sh_ref, h_ref):
    scale = 1.0 + sc_ref[...]
    shift = sh_ref[...]

    def body(rows):
        h_ref[rows, :] = (_normalise(x_ref[rows, :]) * scale + shift).astype(h_ref.dtype)

    _row_chunks(x_ref.shape[0], body)


def _mod_spec(d, row_of_tile, which):
    return pl.BlockSpec((None, 1, d), lambda i: (row_of_tile(i), 0, which))


def ln_modulate(x, mod, row_of, which_scale, which_shift):
    m, d = x.shape
    tm = _tile(m, 512)
    row = lambda i: row_of(i * tm)
    return pl.pallas_call(
        _ln_mod_kernel,
        grid=(m // tm,),
        in_specs=[pl.BlockSpec((tm, d), lambda i: (i, 0)),
                  _mod_spec(d, row, which_scale), _mod_spec(d, row, which_shift)],
        out_specs=pl.BlockSpec((tm, d), lambda i: (i, 0)),
        out_shape=jax.ShapeDtypeStruct((m, d), BF16),
        compiler_params=_cparams(("arbitrary",), 40),
        name="ln_modulate",
    )(x, mod, mod)


def _res_ln_kernel(*refs, alpha, with_next):
    if with_next:
        x_ref, y_ref, g_ref, gam_ref, bet_ref, sc_ref, sh_ref, xo_ref, h_ref = refs
        scale = 1.0 + sc_ref[...]
        shift = sh_ref[...]
    else:
        x_ref, y_ref, g_ref, gam_ref, bet_ref, xo_ref = refs
    gate, gamma, beta = g_ref[...], gam_ref[...], bet_ref[...]

    def body(rows):
        z = alpha * x_ref[rows, :] + gate * y_ref[rows, :].astype(F32)
        xn = _normalise(z) * gamma + beta
        xo_ref[rows, :] = xn
        if with_next:
            h_ref[rows, :] = (_normalise(xn) * scale + shift).astype(h_ref.dtype)

    _row_chunks(x_ref.shape[0], body)


def residual_ln(x, y, mod, row_of, which_gate, gamma, beta, alpha, nxt=None):
    m, d = x.shape
    tm = _tile(m, 256)
    row = lambda i: row_of(i * tm)
    tile = pl.BlockSpec((tm, d), lambda i: (i, 0))
    vec = pl.BlockSpec((1, d), lambda i: (0, 0))
    in_specs = [tile, tile, _mod_spec(d, row, which_gate), vec, vec]
    args = [x, y, mod, gamma.reshape(1, d), beta.reshape(1, d)]
    out_specs = [tile]
    out_shape = [jax.ShapeDtypeStruct((m, d), F32)]
    if nxt is not None:
        in_specs += [_mod_spec(d, row, nxt[1]), _mod_spec(d, row, nxt[2])]
        args += [nxt[0], nxt[0]]
        out_specs.append(tile)
        out_shape.append(jax.ShapeDtypeStruct((m, d), BF16))
    out = pl.pallas_call(
        functools.partial(_res_ln_kernel, alpha=alpha, with_next=nxt is not None),
        grid=(m // tm,),
        in_specs=in_specs,
        out_specs=out_specs,
        out_shape=out_shape,
        compiler_params=_cparams(("arbitrary",), 48),
        name="residual_ln",
    )(*args)
    return (out[0], out[1]) if nxt is not None else (out[0], None)


def _mm_kernel(*refs, n_parts, scale, head_major):
    a_refs, w_refs, o_ref = refs[:n_parts], refs[n_parts:2 * n_parts], refs[2 * n_parts]
    acc = None
    for a_ref, w_ref in zip(a_refs, w_refs):
        part = jnp.dot(a_ref[...], w_ref[...], preferred_element_type=F32)
        acc = part if acc is None else acc + part
    if scale is not None:
        acc = acc * scale
    if not head_major:
        o_ref[...] = acc.astype(o_ref.dtype)
        return
    bt, nh, lb, dh = o_ref.shape
    for b in range(bt):
        for h in range(nh):
            o_ref[b, h] = acc[b * lb:(b + 1) * lb, h * dh:(h + 1) * dh].astype(o_ref.dtype)


def matmul(a_parts, w, layer, *, n, col_off=0, out_dtype=BF16, scale=None, seq_len=None, tm=1024, tn=1024):
    m, kp = a_parts[0].shape
    assert all(a.shape == (m, kp) for a in a_parts) and w.shape[1] == kp * len(a_parts)
    tm, tn = _tile(m, tm), _tile(n, tn)
    assert col_off % tn == 0
    joff = col_off // tn
    in_specs = [pl.BlockSpec((tm, kp), lambda i, j: (i, 0)) for _ in a_parts]
    in_specs += [pl.BlockSpec((None, kp, tn), functools.partial(lambda i, j, p: (layer, p, j + joff), p=p))
                 for p in range(len(a_parts))]
    if seq_len is None:
        out_spec = pl.BlockSpec((tm, tn), lambda i, j: (i, j))
        out_shape = jax.ShapeDtypeStruct((m, n), out_dtype)
    else:
        nh = tn // HEAD_DIM
        out_shape = jax.ShapeDtypeStruct((m // seq_len, n // HEAD_DIM, seq_len, HEAD_DIM), out_dtype)
        if tm >= seq_len:
            out_spec = pl.BlockSpec((tm // seq_len, nh, seq_len, HEAD_DIM), lambda i, j: (i, j, 0, 0))
        else:
            per = seq_len // tm
            out_spec = pl.BlockSpec((1, nh, tm, HEAD_DIM), lambda i, j: (i // per, j, i % per, 0))
    return pl.pallas_call(
        functools.partial(_mm_kernel, n_parts=len(a_parts), scale=scale, head_major=seq_len is not None),
        grid=(m // tm, n // tn),
        in_specs=in_specs,
        out_specs=out_spec,
        out_shape=out_shape,
        compiler_params=_cparams(("arbitrary", "arbitrary"), 56),
        name="matmul",
    )(*a_parts, *([w] * len(a_parts)))


def _mm_ksplit_kernel(a_ref, w_ref, o_ref, acc_ref, *, k_steps):
    part = jnp.dot(a_ref[...], w_ref[...], preferred_element_type=F32)
    s = pl.program_id(2)

    @pl.when(s == 0)
    def _():
        acc_ref[...] = part

    @pl.when((s > 0) & (s < k_steps - 1))
    def _():
        acc_ref[...] += part

    @pl.when(s == k_steps - 1)
    def _():
        o_ref[...] = (acc_ref[...] + part).astype(o_ref.dtype)


def matmul_ksplit(a, w, layer, *, out_dtype=BF16, tm=1024, tn=512, k_steps=2):
    m, k = a.shape
    n = w.shape[2]
    tm, tn = _tile(m, tm), _tile(n, tn)
    tk = k // k_steps
    assert k_steps >= 2 and tk * k_steps == k and tk % 128 == 0
    return pl.pallas_call(
        functools.partial(_mm_ksplit_kernel, k_steps=k_steps),
        grid=(m // tm, n // tn, k_steps),
        in_specs=[pl.BlockSpec((tm, tk), lambda i, j, s: (i, s)),
                  pl.BlockSpec((None, tk, tn), lambda i, j, s: (layer, s, j))],
        out_specs=pl.BlockSpec((tm, tn), lambda i, j, s: (i, j)),
        out_shape=jax.ShapeDtypeStruct((m, n), out_dtype),
        scratch_shapes=[pltpu.VMEM((tm, tn), F32)],
        compiler_params=_cparams(("arbitrary", "arbitrary", "arbitrary"), 56),
        name="matmul_ksplit",
    )(a, w)


def _ffn_up_kernel(a_ref, halo_ref, wa_ref, wg_ref, cwa_ref, cwg_ref, o_ref, *, seq_len):
    a = a_ref[...]
    halo = halo_ref[...]
    tm, tn = o_ref.shape
    sub = lax.broadcasted_iota(jnp.int32, (8, tn), 0)
    inner = range(seq_len, tm, seq_len)

    def patch(x, group, row, value):
        lo, hi = group * 8, (group + 1) * 8
        parts = [x[:lo]] * (lo > 0) + [jnp.where(sub == row, value, x[lo:hi])] + [x[hi:]] * (hi < tm)
        return jnp.concatenate(parts, axis=0)

    def conv(w_ref, cw_ref):
        w = w_ref[...]
        u = jnp.dot(a, w, preferred_element_type=F32)
        uh = jnp.dot(halo, w, preferred_element_type=F32)
        prev = patch(pltpu.roll(u, 1, 0), 0, 0, uh[7:8])
        nxt = patch(pltpu.roll(u, tm - 1, 0), tm // 8 - 1, 7, uh[0:1])
        for r in inner:
            prev = patch(prev, r // 8, 0, 0.0)
            nxt = patch(nxt, r // 8 - 1, 7, 0.0)
        cw = cw_ref[...]
        return prev * cw[0:1] + u * cw[1:2] + nxt * cw[2:3] + cw[3:4]

    act = conv(wa_ref, cwa_ref)
    half = 0.5 * conv(wg_ref, cwg_ref)
    o_ref[...] = ((half + half * jnp.tanh(half)) * act).astype(o_ref.dtype)


def ffn_up(h, w_up, conv_w, conv_b, layer, seq_len, *, tm=2048, tn=256):
    m, d = h.shape
    d_ff = w_up.shape[2] // 2
    tm, tn = _tile(m, tm), _tile(d_ff, tn)
    assert (tm % seq_len == 0 or seq_len % tm == 0) and tm % 16 == 0
    nt, nj = m // tm, d_ff // tn
    tiles = h.reshape(nt, tm, d)
    zero = jnp.zeros((1, d), h.dtype)
    starts = np.arange(nt) * tm
    before = jnp.concatenate([zero, tiles[:-1, -1]], axis=0)
    before = jnp.where(jnp.asarray(starts % seq_len == 0)[:, None], 0, before)
    after = jnp.concatenate([tiles[1:, 0], zero], axis=0)
    after = jnp.where(jnp.asarray((starts + tm) % seq_len == 0)[:, None], 0, after)
    halo = jnp.concatenate([after[:, None], jnp.zeros((nt, 6, d), h.dtype), before[:, None]], axis=1)
    taps = jnp.concatenate([conv_w[layer], conv_b[layer][None]], axis=0)
    return pl.pallas_call(
        functools.partial(_ffn_up_kernel, seq_len=seq_len),
        grid=(nt, nj),
        in_specs=[
            pl.BlockSpec((tm, d), lambda i, j: (i, 0)),
            pl.BlockSpec((8, d), lambda i, j: (i, 0)),
            pl.BlockSpec((None, d, tn), lambda i, j: (layer, 0, j)),
            pl.BlockSpec((None, d, tn), lambda i, j: (layer, 0, j + nj)),
            pl.BlockSpec((4, tn), lambda i, j: (0, j)),
            pl.BlockSpec((4, tn), lambda i, j: (0, j + nj)),
        ],
        out_specs=pl.BlockSpec((tm, tn), lambda i, j: (i, j)),
        out_shape=jax.ShapeDtypeStruct((m, d_ff), BF16),
        compiler_params=_cparams(("arbitrary", "arbitrary"), 56),
        name="ffn_up",
    )(h, halo.reshape(nt * 8, d), w_up, w_up, taps, taps)


def _dft_matrices(n):
    k = jnp.arange(n, dtype=jnp.int32)
    phase = ((k[:, None] * k[None, :]) % n).astype(F32) * (2.0 * np.pi / n)
    scale = 1.0 / np.sqrt(n)
    return (jnp.cos(phase) * scale).astype(BF16), (jnp.sin(phase) * scale).astype(BF16)


def _fourier_kernel(u_ref, ch_ref, sh_ref, cmid_ref, flip_ref, cc_ref, sc_ref, w_ref, o_ref):
    n = u_ref.shape[0]
    hn = n // 2
    gw = cc_ref.shape[0]
    sub = lax.broadcasted_iota(jnp.int32, (8, gw), 0)
    for g in range(w_ref.shape[0]):
        cols = slice(g * gw, (g + 1) * gw)
        u = u_ref[:, cols]
        zc = jnp.dot(u, cc_ref[...], preferred_element_type=F32).astype(BF16)
        zs = jnp.dot(u, sc_ref[...], preferred_element_type=F32).astype(BF16)
        p1 = jnp.dot(ch_ref[...], zc, preferred_element_type=F32)
        p2 = jnp.dot(sh_ref[...], zs, preferred_element_type=F32)
        mid = jnp.dot(cmid_ref[...], zc, preferred_element_type=F32)
        w = w_ref[g]
        o_ref[0:hn, cols] = jnp.dot((p1 - p2).astype(BF16), w, preferred_element_type=F32).astype(o_ref.dtype)
        back = jnp.dot(flip_ref[...], (p1 + p2).astype(BF16), preferred_element_type=F32)
        back = jnp.concatenate([jnp.where(sub == 0, mid[0:1], back[0:8]), back[8:]], axis=0)
        o_ref[hn:n, cols] = jnp.dot(back.astype(BF16), w, preferred_element_type=F32).astype(o_ref.dtype)


def fourier_mix(u, w_four, seq_len):
    m = u.shape[0]
    groups, gw, _ = w_four.shape
    gps = groups if seq_len <= SHORT_SEQ else 1
    hn = seq_len // 2
    assert hn % 8 == 0
    cl, sl = _dft_matrices(seq_len)
    cc, sc = _dft_matrices(gw)
    r = jnp.arange(hn, dtype=jnp.int32)
    flip = ((r[:, None] + r[None, :] == hn) & (r[:, None] >= 1)).astype(BF16)
    const = lambda shape, idx: pl.BlockSpec(shape, lambda b, g: idx, pipeline_mode=pl.Buffered(1))
    return pl.pallas_call(
        _fourier_kernel,
        grid=(m // seq_len, groups // gps),
        in_specs=[
            pl.BlockSpec((seq_len, gw * gps), lambda b, g: (b, g)),
            const((hn, seq_len), (0, 0)), const((hn, seq_len), (0, 0)), const((8, seq_len), (hn // 8, 0)),
            const((hn, hn), (0, 0)), const((gw, gw), (0, 0)), const((gw, gw), (0, 0)),
            pl.BlockSpec((gps, gw, gw), lambda b, g: (g, 0, 0)),
        ],
        out_specs=pl.BlockSpec((seq_len, gw * gps), lambda b, g: (b, g)),
        out_shape=jax.ShapeDtypeStruct((m, groups * gw), BF16),
        compiler_params=_cparams(("arbitrary", "arbitrary"), 56),
        name="fourier_mix",
    )(u, cl, sl, cl, flip, cc, sc, w_four)


def _pool_matrices(seq_len):
    t = jnp.arange(seq_len, dtype=jnp.int32)
    bands, inv = [], []
    for w in POOL_WINDOWS:
        lo = jnp.clip(t - w // 2, 0, seq_len)
        hi = jnp.clip(t - w // 2 + w, 0, seq_len)
        bands.append(((t[None, :] >= lo[:, None]) & (t[None, :] < hi[:, None])).astype(BF16))
        inv.append(1.0 / (hi - lo).astype(F32))
    return jnp.stack(bands), jnp.stack(inv)[:, :, None]


def _pool_kernel(u_ref, band_ref, inv_ref, w_ref, ps_ref, o_ref):
    seq_len = u_ref.shape[0]
    rows = min(POOL_ROWS, seq_len)
    span = min(2 * rows, seq_len)
    assert seq_len % rows == 0 and max(POOL_WINDOWS) <= rows // 2
    gw = w_ref.shape[-1]
    for g in range(w_ref.shape[0]):
        cols = slice(g * gw, (g + 1) * gw)
        for r0 in range(0, seq_len, rows):
            c0 = min(max(r0 - rows // 2, 0), seq_len - span)
            u = u_ref[c0:c0 + span, cols]
            sums = jnp.dot(band_ref[g, r0:r0 + rows, c0:c0 + span], u, preferred_element_type=F32)
            pooled = sums * inv_ref[g, r0:r0 + rows, :] - u_ref[r0:r0 + rows, cols].astype(F32)
            y = jnp.dot(pooled.astype(BF16), w_ref[g], preferred_element_type=F32)
            o_ref[r0:r0 + rows, cols] = (y * ps_ref[:, cols]).astype(o_ref.dtype)


def pool_mix(u, w_pool, pool_scale, seq_len, group_off):
    m = u.shape[0]
    groups, gw, _ = w_pool.shape
    gps = groups if seq_len <= SHORT_SEQ else 1
    assert group_off % gps == 0
    band, inv = _pool_matrices(seq_len)
    return pl.pallas_call(
        _pool_kernel,
        grid=(groups // gps, m // seq_len),
        in_specs=[
            pl.BlockSpec((seq_len, gw * gps), lambda g, b: (b, g + group_off // gps)),
            pl.BlockSpec((gps, seq_len, seq_len), lambda g, b: (g, 0, 0)),
            pl.BlockSpec((gps, seq_len, 1), lambda g, b: (g, 0, 0)),
            pl.BlockSpec((gps, gw, gw), lambda g, b: (g, 0, 0)),
            pl.BlockSpec((1, gw * gps), lambda g, b: (0, g)),
        ],
        out_specs=pl.BlockSpec((seq_len, gw * gps), lambda g, b: (b, g)),
        out_shape=jax.ShapeDtypeStruct((m, groups * gw), BF16),
        compiler_params=_cparams(("arbitrary", "arbitrary"), 48),
        name="pool_mix",
    )(u, band, inv, w_pool, pool_scale.reshape(1, groups * gw))


_NT_DIMS = (((1,), (1,)), ((), ()))


def _ctx_attn_kernel(q_ref, k_ref, v_ref, o_ref):
    n_heads, _, dh = q_ref.shape
    for h in range(n_heads):
        k = k_ref[h].astype(BF16)
        v = v_ref[h].astype(BF16)
        s = lax.dot_general(q_ref[h], k, _NT_DIMS, preferred_element_type=F32)
        p = jnp.exp(s - jnp.max(s, axis=-1, keepdims=True))
        denom = jnp.sum(p, axis=-1, keepdims=True)
        o = jnp.dot(p.astype(BF16), v, preferred_element_type=F32) / denom
        o_ref[:, h * dh:(h + 1) * dh] = o.astype(o_ref.dtype)


def context_attention(q, k, v):
    b, n_heads, seq_len, dh = q.shape
    blk = pl.BlockSpec((None, n_heads, seq_len, dh), lambda i: (i, 0, 0, 0))
    return pl.pallas_call(
        _ctx_attn_kernel,
        grid=(b,),
        in_specs=[blk, blk, blk],
        out_specs=pl.BlockSpec((seq_len, n_heads * dh), lambda i: (i, 0)),
        out_shape=jax.ShapeDtypeStruct((b * seq_len, n_heads * dh), BF16),
        compiler_params=_cparams(("arbitrary",), 48),
        name="context_attention",
    )(q, k, v)


def _na_structure(rows):
    kr = min(NA_ROWS, rows)
    assert rows % NA_TILE_ROWS == 0 and rows >= NA_WIN_ROWS
    n_tiles = rows // NA_TILE_ROWS
    starts = np.clip(np.arange(n_tiles) * NA_TILE_ROWS - kr // 2, 0, rows - NA_WIN_ROWS)
    patterns, pattern_of = [], []
    for t in range(n_tiles):
        r = t * NA_TILE_ROWS + np.arange(NA_TILE_ROWS)[:, None]
        ka = starts[t] + np.arange(NA_WIN_ROWS)[None, :]
        rs = np.clip(r - kr // 2, 0, rows - kr)
        assert (rs >= starts[t]).all() and (rs + kr <= starts[t] + NA_WIN_ROWS).all()
        slot = np.where((ka >= rs) & (ka < rs + kr), ka - r + NA_ROWS - 1, 2 * NA_ROWS - 1)
        for p, known in enumerate(patterns):
            if (known == slot).all():
                pattern_of.append(p)
                break
        else:
            pattern_of.append(len(patterns))
            patterns.append(slot)
    return tuple(int(s) for s in starts), tuple(pattern_of), np.stack(patterns)


def _na_bias(rpb, dr_slot):
    n_heads = rpb.shape[0]
    qc = np.arange(GRID_W)[:, None]
    kc = np.arange(GRID_W)[None, :]
    col_start = np.clip(qc - NA_COLS // 2, 0, GRID_W - NA_COLS)
    col_ok = (kc >= col_start) & (kc < col_start + NA_COLS)
    dc = np.clip(kc - qc + NA_COLS - 1, 0, 2 * NA_COLS - 2)
    onehot = (dc.reshape(1, -1) == np.arange(2 * NA_COLS - 1)[:, None]).astype(np.float32)
    toep = jnp.einsum("hds,sn->hdn", rpb, jnp.asarray(onehot), precision=lax.Precision.HIGHEST)
    toep = jnp.where(jnp.asarray(col_ok)[None, None], toep.reshape(n_heads, 2 * NA_ROWS - 1, GRID_W, GRID_W), NEG)
    toep = jnp.concatenate([toep, jnp.full((n_heads, 1, GRID_W, GRID_W), NEG, F32)], axis=1)
    n_pat = dr_slot.shape[0]
    strips = [jnp.concatenate([toep[:, int(s)] for s in dr_slot[p, qr]], axis=-1)
              for p in range(n_pat) for qr in range(NA_TILE_ROWS)]
    return jnp.stack(strips, axis=1).reshape(n_heads, n_pat, NA_TILE_ROWS * GRID_W, NA_WIN_ROWS * GRID_W)


def _na_kernel(q_ref, k_ref, v_ref, ck_ref, cv_ref, bias_ref, o_ref, *, starts, pattern_of):
    tq = NA_TILE_ROWS * GRID_W
    tk = NA_WIN_ROWS * GRID_W
    ck = ck_ref[...].astype(BF16)
    cv = cv_ref[...].astype(BF16)

    def scores(t):
        q = q_ref[t * tq:(t + 1) * tq, :]
        kw = k_ref[starts[t] * GRID_W:starts[t] * GRID_W + tk, :]
        s_loc = lax.dot_general(q, kw, _NT_DIMS, preferred_element_type=F32) + bias_ref[pattern_of[t]]
        s_ctx = lax.dot_general(q, ck, _NT_DIMS, preferred_element_type=F32)
        return s_loc, s_ctx

    def finish(t, s_loc, s_ctx):
        vw = v_ref[starts[t] * GRID_W:starts[t] * GRID_W + tk, :]
        mx = jnp.maximum(jnp.max(s_loc, axis=-1, keepdims=True), jnp.max(s_ctx, axis=-1, keepdims=True))
        p_loc = jnp.exp(s_loc - mx)
        p_ctx = jnp.exp(s_ctx - mx)
        denom = jnp.sum(p_loc, axis=-1, keepdims=True) + jnp.sum(p_ctx, axis=-1, keepdims=True)
        o = (jnp.dot(p_loc.astype(BF16), vw, preferred_element_type=F32)
             + jnp.dot(p_ctx.astype(BF16), cv, preferred_element_type=F32)) / denom
        o_ref[t * tq:(t + 1) * tq, :] = o.astype(o_ref.dtype)

    n_tiles = len(starts)
    sc = scores(0)
    for t in range(n_tiles):
        nxt = scores(t + 1) if t + 1 < n_tiles else None
        finish(t, *sc)
        sc = nxt


def neighbourhood_attention(q, k, v, ctx_k, ctx_v, layer, rpb):
    b, n_heads, seq_len, dh = q.shape
    lc = ctx_k.shape[3]
    starts, pattern_of, dr_slot = _na_structure(seq_len // GRID_W)
    bias = _na_bias(rpb, dr_slot)
    _, n_pat, tq, tk = bias.shape
    qkv = pl.BlockSpec((None, None, seq_len, dh), lambda h, i: (i, h, 0, 0))
    ctx = pl.BlockSpec((None, None, None, lc, dh), lambda h, i: (i, layer, h, 0, 0))
    return pl.pallas_call(
        functools.partial(_na_kernel, starts=starts, pattern_of=pattern_of),
        grid=(n_heads, b),
        in_specs=[qkv, qkv, qkv, ctx, ctx,
                  pl.BlockSpec((None, n_pat, tq, tk), lambda h, i: (h, 0, 0, 0))],
        out_specs=pl.BlockSpec((seq_len, dh), lambda h, i: (i, h)),
        out_shape=jax.ShapeDtypeStruct((b * seq_len, n_heads * dh), BF16),
        compiler_params=_cparams(("arbitrary", "arbitrary"), 48),
        name="neighbourhood_attention",
    )(q, k, v, ctx_k, ctx_v, bias)


SHIFT1, SCALE1, GATE1, SHIFT2, SCALE2, GATE2 = range(6)


def _trunk(x, seq_len, mod, row_of, ctx_kv, wts, alpha):
    (ln1_g, ln1_b, ln2_g, ln2_b, w_in, w_four, w_pool, pool_scale, w_out_a, w_qkv, rpb, w_out_c,
     w_up, conv_w, conv_b, w_down) = wts
    depth = mod.shape[0]
    d = x.shape[1]
    new_k, new_v = [], []
    h = ln_modulate(x, mod[0], row_of, SCALE1, SHIFT1)
    for i in range(depth):
        j = i // 2
        if i % 2 == 0:
            u = matmul([h], w_in, j, n=d)
            ya = fourier_mix(u, w_four[j], seq_len)
            yb = pool_mix(u, w_pool[j], pool_scale[j], seq_len, w_four.shape[1])
            y = matmul([ya, yb], w_out_a, j, n=d)
        else:
            q = matmul([h], w_qkv, j, n=d, col_off=0, scale=HEAD_DIM ** -0.5, seq_len=seq_len)
            if ctx_kv is None:
                k = matmul([h], w_qkv, j, n=d, col_off=d, out_dtype=F32, seq_len=seq_len)
                v = matmul([h], w_qkv, j, n=d, col_off=2 * d, out_dtype=F32, seq_len=seq_len)
                new_k.append(k)
                new_v.append(v)
                o = context_attention(q, k, v)
            else:
                k = matmul([h], w_qkv, j, n=d, col_off=d, seq_len=seq_len)
                v = matmul([h], w_qkv, j, n=d, col_off=2 * d, seq_len=seq_len)
                o = neighbourhood_attention(q, k, v, ctx_kv[0], ctx_kv[1], j, rpb[j])
            y = matmul([o], w_out_c, j, n=d)
        x, h = residual_ln(x, y, mod[i], row_of, GATE1, ln1_g[i], ln1_b[i], alpha, nxt=(mod[i], SCALE2, SHIFT2))
        f = matmul_ksplit(ffn_up(h, w_up, conv_w, conv_b, i, seq_len), w_down, i)
        nxt = (mod[i + 1], SCALE1, SHIFT1) if i + 1 < depth else None
        x, h = residual_ln(x, f, mod[i], row_of, GATE2, ln2_g[i], ln2_b[i], alpha, nxt=nxt)
    return x, new_k, new_v


def kernel(x_prompt, x_sample, cache_k, cache_v, c, c_ctx, w_ada, b_ada, ln1_g, ln1_b, ln2_g, ln2_b,
           w_in, w_four, w_pool, pool_scale, w_out_a, w_qkv, rpb, w_out_c, w_up, conv_w, conv_b, w_down):
    batch, seq, d = x_prompt.shape
    dec_batch, dec_seq, _ = x_sample.shape
    depth = w_ada.shape[0]
    alpha = float((2 * depth) ** 0.25)
    assert dec_batch < COND_ROWS

    cond = jnp.concatenate([c, c_ctx[None, :], jnp.zeros((COND_ROWS - dec_batch - 1, d), F32)], axis=0)
    mod = ada_modulation(cond, w_ada, b_ada).reshape(depth, COND_ROWS, 1, 6 * d)

    wts = (ln1_g, ln1_b, ln2_g, ln2_b,
           w_in.astype(BF16), w_four.astype(BF16), w_pool.astype(BF16), pool_scale, w_out_a.astype(BF16),
           w_qkv.astype(BF16), rpb, w_out_c.astype(BF16),
           w_up.astype(BF16), conv_w, conv_b, w_down.astype(BF16))

    y_prompt, ks, vs = _trunk(x_prompt.reshape(batch * seq, d), seq, mod, lambda r: dec_batch, None, wts, alpha)
    y_sample, _, _ = _trunk(x_sample.reshape(dec_batch * dec_seq, d), dec_seq, mod, lambda r: r // dec_seq,
                            (cache_k, cache_v), wts, alpha)
    return (y_prompt.reshape(batch, seq, d), y_sample.reshape(dec_batch, dec_seq, d),
            jnp.stack(ks, axis=1), jnp.stack(vs, axis=1))
```

```python
import functools

import numpy as np
import jax
import jax.numpy as jnp
from jax import lax
from jax.experimental import pallas as pl
from jax.experimental.pallas import tpu as pltpu

F32 = jnp.float32
BF16 = jnp.bfloat16

LN_EPS = 1e-5
NEG = -1e30
GRID_W = 64
NA_ROWS = 8
NA_COLS = 16
HEAD_DIM = 128
POOL_WINDOWS = (2, 4, 8, 16)
COND_ROWS = 16
NA_TILE_ROWS = 4
NA_WIN_ROWS = 12
LN_CHUNK = 64
POOL_ROWS = 256
SHORT_SEQ = 512
MIB = 1024 * 1024


def _cparams(semantics, vmem_mib):
    return pltpu.CompilerParams(dimension_semantics=semantics, vmem_limit_bytes=int(vmem_mib * MIB))


def _tile(dim, pref):
    t = min(dim, pref)
    assert dim % t == 0, (dim, pref)
    return t


def _ada_kernel(c_ref, w_ref, b_ref, o_ref):
    c = c_ref[...]
    s = (c * jax.nn.sigmoid(c)).astype(BF16)
    o_ref[...] = jnp.dot(s, w_ref[...].astype(BF16), preferred_element_type=F32) + b_ref[...]


def ada_modulation(cond, w_ada, b_ada):
    depth, d, n = w_ada.shape
    tn = _tile(n, 512)
    return pl.pallas_call(
        _ada_kernel,
        grid=(depth, n // tn),
        in_specs=[
            pl.BlockSpec((COND_ROWS, d), lambda l, j: (0, 0)),
            pl.BlockSpec((None, d, tn), lambda l, j: (l, 0, j)),
            pl.BlockSpec((None, 1, tn), lambda l, j: (l, 0, j)),
        ],
        out_specs=pl.BlockSpec((None, COND_ROWS, tn), lambda l, j: (l, 0, j)),
        out_shape=jax.ShapeDtypeStruct((depth, COND_ROWS, n), F32),
        compiler_params=_cparams(("arbitrary", "arbitrary"), 40),
        name="ada_modulation",
    )(cond, w_ada, b_ada.reshape(depth, 1, n))


def _normalise(x):
    mu = jnp.mean(x, axis=-1, keepdims=True)
    xc = x - mu
    var = jnp.mean(xc * xc, axis=-1, keepdims=True)
    return xc * lax.rsqrt(var + LN_EPS)


def _row_chunks(n_rows, body):
    chunk = min(LN_CHUNK, n_rows)
    assert n_rows % chunk == 0

    def step(c, carry):
        body(pl.ds(pl.multiple_of(c * chunk, chunk), chunk))
        return carry

    lax.fori_loop(0, n_rows // chunk, step, 0)


def _ln_mod_kernel(x_ref, sc_ref, sh_ref, h_ref):
    scale = 1.0 + sc_ref[...]
    shift = sh_ref[...]

    def body(rows):
        h_ref[rows, :] = (_normalise(x_ref[rows, :]) * scale + shift).astype(h_ref.dtype)

    _row_chunks(x_ref.shape[0], body)


def _mod_spec(d, row_of_tile, which):
    return pl.BlockSpec((None, 1, d), lambda i: (row_of_tile(i), 0, which))


def ln_modulate(x, mod, row_of, which_scale, which_shift):
    m, d = x.shape
    tm = _tile(m, 512)
    row = lambda i: row_of(i * tm)
    return pl.pallas_call(
        _ln_mod_kernel,
        grid=(m // tm,),
        in_specs=[pl.BlockSpec((tm, d), lambda i: (i, 0)),
                  _mod_spec(d, row, which_scale), _mod_spec(d, row, which_shift)],
        out_specs=pl.BlockSpec((tm, d), lambda i: (i, 0)),
        out_shape=jax.ShapeDtypeStruct((m, d), BF16),
        compiler_params=_cparams(("arbitrary",), 40),
        name="ln_modulate",
    )(x, mod, mod)


def _res_ln_kernel(*refs, alpha, with_next):
    if with_next:
        x_ref, y_ref, g_ref, gam_ref, bet_ref, sc_ref, sh_ref, xo_ref, h_ref = refs
        scale = 1.0 + sc_ref[...]
        shift = sh_ref[...]
    else:
        x_ref, y_ref, g_ref, gam_ref, bet_ref, xo_ref = refs
    gate, gamma, beta = g_ref[...], gam_ref[...], bet_ref[...]

    def body(rows):
        z = alpha * x_ref[rows, :] + gate * y_ref[rows, :].astype(F32)
        xn = _normalise(z) * gamma + beta
        xo_ref[rows, :] = xn
        if with_next:
            h_ref[rows, :] = (_normalise(xn) * scale + shift).astype(h_ref.dtype)

    _row_chunks(x_ref.shape[0], body)


def residual_ln(x, y, mod, row_of, which_gate, gamma, beta, alpha, nxt=None):
    m, d = x.shape
    tm = _tile(m, 512)
    row = lambda i: row_of(i * tm)
    tile = pl.BlockSpec((tm, d), lambda i: (i, 0))
    vec = pl.BlockSpec((1, d), lambda i: (0, 0))
    in_specs = [tile, tile, _mod_spec(d, row, which_gate), vec, vec]
    args = [x, y, mod, gamma.reshape(1, d), beta.reshape(1, d)]
    out_specs = [tile]
    out_shape = [jax.ShapeDtypeStruct((m, d), F32)]
    if nxt is not None:
        in_specs += [_mod_spec(d, row, nxt[1]), _mod_spec(d, row, nxt[2])]
        args += [nxt[0], nxt[0]]
        out_specs.append(tile)
        out_shape.append(jax.ShapeDtypeStruct((m, d), BF16))
    out = pl.pallas_call(
        functools.partial(_res_ln_kernel, alpha=alpha, with_next=nxt is not None),
        grid=(m // tm,),
        in_specs=in_specs,
        out_specs=out_specs,
        out_shape=out_shape,
        compiler_params=_cparams(("arbitrary",), 56),
        name="residual_ln",
    )(*args)
    return (out[0], out[1]) if nxt is not None else (out[0], None)


def _mm_kernel(*refs, n_parts, scale, scaled_tiles, head_major):
    a_refs, w_refs, o_ref = refs[:n_parts], refs[n_parts:2 * n_parts], refs[2 * n_parts]
    acc = None
    for a_ref, w_ref in zip(a_refs, w_refs):
        part = jnp.dot(a_ref[...], w_ref[...], preferred_element_type=F32)
        acc = part if acc is None else acc + part
    if scale is not None:
        acc = acc * jnp.where(pl.program_id(1) < scaled_tiles, scale, 1.0)
    if not head_major:
        o_ref[...] = acc.astype(o_ref.dtype)
        return
    bt, nh, lb, dh = o_ref.shape
    for b in range(bt):
        for h in range(nh):
            o_ref[b, h] = acc[b * lb:(b + 1) * lb, h * dh:(h + 1) * dh].astype(o_ref.dtype)


def matmul(a_parts, w, layer, *, n, col_off=0, out_dtype=BF16, scale=None, scaled_cols=None, seq_len=None,
           tm=1024, tn=1024):
    m, kp = a_parts[0].shape
    assert all(a.shape == (m, kp) for a in a_parts) and w.shape[1] == kp * len(a_parts)
    tm, tn = _tile(m, tm), _tile(n, tn)
    scaled_cols = n if scaled_cols is None else scaled_cols
    assert col_off % tn == 0 and scaled_cols % tn == 0
    joff = col_off // tn
    in_specs = [pl.BlockSpec((tm, kp), lambda i, j: (i, 0)) for _ in a_parts]
    in_specs += [pl.BlockSpec((None, kp, tn), functools.partial(lambda i, j, p: (layer, p, j + joff), p=p))
                 for p in range(len(a_parts))]
    if seq_len is None:
        out_spec = pl.BlockSpec((tm, tn), lambda i, j: (i, j))
        out_shape = jax.ShapeDtypeStruct((m, n), out_dtype)
    else:
        nh = tn // HEAD_DIM
        out_shape = jax.ShapeDtypeStruct((m // seq_len, n // HEAD_DIM, seq_len, HEAD_DIM), out_dtype)
        if tm >= seq_len:
            out_spec = pl.BlockSpec((tm // seq_len, nh, seq_len, HEAD_DIM), lambda i, j: (i, j, 0, 0))
        else:
            per = seq_len // tm
            out_spec = pl.BlockSpec((1, nh, tm, HEAD_DIM), lambda i, j: (i // per, j, i % per, 0))
    return pl.pallas_call(
        functools.partial(_mm_kernel, n_parts=len(a_parts), scale=scale, scaled_tiles=scaled_cols // tn,
                          head_major=seq_len is not None),
        grid=(m // tm, n // tn),
        in_specs=in_specs,
        out_specs=out_spec,
        out_shape=out_shape,
        compiler_params=_cparams(("arbitrary", "arbitrary"), 56),
        name="matmul",
    )(*a_parts, *([w] * len(a_parts)))


def _mm_ksplit_kernel(a_ref, w_ref, o_ref, acc_ref, *, k_steps):
    part = jnp.dot(a_ref[...], w_ref[...], preferred_element_type=F32)
    s = pl.program_id(2)

    @pl.when(s == 0)
    def _():
        acc_ref[...] = part

    @pl.when((s > 0) & (s < k_steps - 1))
    def _():
        acc_ref[...] += part

    @pl.when(s == k_steps - 1)
    def _():
        o_ref[...] = (acc_ref[...] + part).astype(o_ref.dtype)


def matmul_ksplit(a, w, layer, *, out_dtype=BF16, tm=1024, tn=512, k_steps=2):
    m, k = a.shape
    n = w.shape[2]
    tm, tn = _tile(m, tm), _tile(n, tn)
    tk = k // k_steps
    assert k_steps >= 2 and tk * k_steps == k and tk % 128 == 0
    return pl.pallas_call(
        functools.partial(_mm_ksplit_kernel, k_steps=k_steps),
        grid=(m // tm, n // tn, k_steps),
        in_specs=[pl.BlockSpec((tm, tk), lambda i, j, s: (i, s)),
                  pl.BlockSpec((None, tk, tn), lambda i, j, s: (layer, s, j))],
        out_specs=pl.BlockSpec((tm, tn), lambda i, j, s: (i, j)),
        out_shape=jax.ShapeDtypeStruct((m, n), out_dtype),
        scratch_shapes=[pltpu.VMEM((tm, tn), F32)],
        compiler_params=_cparams(("arbitrary", "arbitrary", "arbitrary"), 56),
        name="matmul_ksplit",
    )(a, w)


def _ffn_up_kernel(a_ref, halo_ref, wa_ref, wg_ref, cwa_ref, cwg_ref, o_ref, *, seq_len):
    a = a_ref[...]
    halo = halo_ref[...]
    tm, tn = o_ref.shape
    sub = lax.broadcasted_iota(jnp.int32, (8, tn), 0)
    inner = range(seq_len, tm, seq_len)

    def patch(x, group, row, value):
        lo, hi = group * 8, (group + 1) * 8
        parts = [x[:lo]] * (lo > 0) + [jnp.where(sub == row, value, x[lo:hi])] + [x[hi:]] * (hi < tm)
        return jnp.concatenate(parts, axis=0)

    def conv(w_ref, cw_ref):
        w = w_ref[...]
        u = jnp.dot(a, w, preferred_element_type=F32)
        uh = jnp.dot(halo, w, preferred_element_type=F32)
        prev = patch(pltpu.roll(u, 1, 0), 0, 0, uh[7:8])
        nxt = patch(pltpu.roll(u, tm - 1, 0), tm // 8 - 1, 7, uh[0:1])
        for r in inner:
            prev = patch(prev, r // 8, 0, 0.0)
            nxt = patch(nxt, r // 8 - 1, 7, 0.0)
        cw = cw_ref[...]
        return prev * cw[0:1] + u * cw[1:2] + nxt * cw[2:3] + cw[3:4]

    act = conv(wa_ref, cwa_ref)
    half = 0.5 * conv(wg_ref, cwg_ref)
    o_ref[...] = ((half + half * jnp.tanh(half)) * act).astype(o_ref.dtype)


def ffn_up(h, w_up, conv_w, conv_b, layer, seq_len, *, tm=2048, tn=256):
    m, d = h.shape
    d_ff = w_up.shape[2] // 2
    tm, tn = _tile(m, tm), _tile(d_ff, tn)
    assert (tm % seq_len == 0 or seq_len % tm == 0) and tm % 16 == 0
    nt, nj = m // tm, d_ff // tn
    tiles = h.reshape(nt, tm, d)
    zero = jnp.zeros((1, d), h.dtype)
    starts = np.arange(nt) * tm
    before = jnp.concatenate([zero, tiles[:-1, -1]], axis=0)
    before = jnp.where(jnp.asarray(starts % seq_len == 0)[:, None], 0, before)
    after = jnp.concatenate([tiles[1:, 0], zero], axis=0)
    after = jnp.where(jnp.asarray((starts + tm) % seq_len == 0)[:, None], 0, after)
    halo = jnp.concatenate([after[:, None], jnp.zeros((nt, 6, d), h.dtype), before[:, None]], axis=1)
    taps = jnp.concatenate([conv_w[layer], conv_b[layer][None]], axis=0)
    return pl.pallas_call(
        functools.partial(_ffn_up_kernel, seq_len=seq_len),
        grid=(nt, nj),
        in_specs=[
            pl.BlockSpec((tm, d), lambda i, j: (i, 0)),
            pl.BlockSpec((8, d), lambda i, j: (i, 0)),
            pl.BlockSpec((None, d, tn), lambda i, j: (layer, 0, j)),
            pl.BlockSpec((None, d, tn), lambda i, j: (layer, 0, j + nj)),
            pl.BlockSpec((4, tn), lambda i, j: (0, j)),
            pl.BlockSpec((4, tn), lambda i, j: (0, j + nj)),
        ],
        out_specs=pl.BlockSpec((tm, tn), lambda i, j: (i, j)),
        out_shape=jax.ShapeDtypeStruct((m, d_ff), BF16),
        compiler_params=_cparams(("arbitrary", "arbitrary"), 56),
        name="ffn_up",
    )(h, halo.reshape(nt * 8, d), w_up, w_up, taps, taps)


def _dft_matrices(n):
    k = jnp.arange(n, dtype=jnp.int32)
    phase = ((k[:, None] * k[None, :]) % n).astype(F32) * (2.0 * np.pi / n)
    scale = 1.0 / np.sqrt(n)
    return (jnp.cos(phase) * scale).astype(BF16), (jnp.sin(phase) * scale).astype(BF16)


def _fourier_kernel(u_ref, ch_ref, sh_ref, cmid_ref, flip_ref, cc_ref, sc_ref, w_ref, o_ref):
    n = u_ref.shape[0]
    hn = n // 2
    gw = cc_ref.shape[0]
    sub = lax.broadcasted_iota(jnp.int32, (8, gw), 0)
    for g in range(w_ref.shape[0]):
        cols = slice(g * gw, (g + 1) * gw)
        u = u_ref[:, cols]
        zc = jnp.dot(u, cc_ref[...], preferred_element_type=F32).astype(BF16)
        zs = jnp.dot(u, sc_ref[...], preferred_element_type=F32).astype(BF16)
        p1 = jnp.dot(ch_ref[...], zc, preferred_element_type=F32)
        p2 = jnp.dot(sh_ref[...], zs, preferred_element_type=F32)
        mid = jnp.dot(cmid_ref[...], zc, preferred_element_type=F32)
        w = w_ref[g]
        o_ref[0:hn, cols] = jnp.dot((p1 - p2).astype(BF16), w, preferred_element_type=F32).astype(o_ref.dtype)
        back = jnp.dot(flip_ref[...], (p1 + p2).astype(BF16), preferred_element_type=F32)
        back = jnp.concatenate([jnp.where(sub == 0, mid[0:1], back[0:8]), back[8:]], axis=0)
        o_ref[hn:n, cols] = jnp.dot(back.astype(BF16), w, preferred_element_type=F32).astype(o_ref.dtype)


def fourier_mix(u, w_four, seq_len):
    m = u.shape[0]
    groups, gw, _ = w_four.shape
    gps = groups if seq_len <= SHORT_SEQ else 1
    hn = seq_len // 2
    assert hn % 8 == 0
    cl, sl = _dft_matrices(seq_len)
    cc, sc = _dft_matrices(gw)
    r = jnp.arange(hn, dtype=jnp.int32)
    flip = ((r[:, None] + r[None, :] == hn) & (r[:, None] >= 1)).astype(BF16)
    const = lambda shape, idx: pl.BlockSpec(shape, lambda b, g: idx, pipeline_mode=pl.Buffered(1))
    return pl.pallas_call(
        _fourier_kernel,
        grid=(m // seq_len, groups // gps),
        in_specs=[
            pl.BlockSpec((seq_len, gw * gps), lambda b, g: (b, g)),
            const((hn, seq_len), (0, 0)), const((hn, seq_len), (0, 0)), const((8, seq_len), (hn // 8, 0)),
            const((hn, hn), (0, 0)), const((gw, gw), (0, 0)), const((gw, gw), (0, 0)),
            pl.BlockSpec((gps, gw, gw), lambda b, g: (g, 0, 0)),
        ],
        out_specs=pl.BlockSpec((seq_len, gw * gps), lambda b, g: (b, g)),
        out_shape=jax.ShapeDtypeStruct((m, groups * gw), BF16),
        compiler_params=_cparams(("arbitrary", "arbitrary"), 56),
        name="fourier_mix",
    )(u, cl, sl, cl, flip, cc, sc, w_four)


def _pool_matrices(seq_len):
    t = jnp.arange(seq_len, dtype=jnp.int32)
    bands, inv = [], []
    for w in POOL_WINDOWS:
        lo = jnp.clip(t - w // 2, 0, seq_len)
        hi = jnp.clip(t - w // 2 + w, 0, seq_len)
        bands.append(((t[None, :] >= lo[:, None]) & (t[None, :] < hi[:, None])).astype(BF16))
        inv.append(1.0 / (hi - lo).astype(F32))
    return jnp.stack(bands), jnp.stack(inv)[:, :, None]


def _pool_kernel(u_ref, band_ref, inv_ref, w_ref, ps_ref, o_ref):
    seq_len = u_ref.shape[0]
    rows = min(POOL_ROWS, seq_len)
    span = min(2 * rows, seq_len)
    assert seq_len % rows == 0 and max(POOL_WINDOWS) <= rows // 2
    gw = w_ref.shape[-1]
    for g in range(w_ref.shape[0]):
        cols = slice(g * gw, (g + 1) * gw)
        for r0 in range(0, seq_len, rows):
            c0 = min(max(r0 - rows // 2, 0), seq_len - span)
            u = u_ref[c0:c0 + span, cols]
            sums = jnp.dot(band_ref[g, r0:r0 + rows, c0:c0 + span], u, preferred_element_type=F32)
            pooled = sums * inv_ref[g, r0:r0 + rows, :] - u_ref[r0:r0 + rows, cols].astype(F32)
            y = jnp.dot(pooled.astype(BF16), w_ref[g], preferred_element_type=F32)
            o_ref[r0:r0 + rows, cols] = (y * ps_ref[:, cols]).astype(o_ref.dtype)


def pool_mix(u, w_pool, pool_scale, seq_len, group_off):
    m = u.shape[0]
    groups, gw, _ = w_pool.shape
    gps = groups if seq_len <= SHORT_SEQ else 1
    assert group_off % gps == 0
    band, inv = _pool_matrices(seq_len)
    return pl.pallas_call(
        _pool_kernel,
        grid=(groups // gps, m // seq_len),
        in_specs=[
            pl.BlockSpec((seq_len, gw * gps), lambda g, b: (b, g + group_off // gps)),
            pl.BlockSpec((gps, seq_len, seq_len), lambda g, b: (g, 0, 0)),
            pl.BlockSpec((gps, seq_len, 1), lambda g, b: (g, 0, 0)),
            pl.BlockSpec((gps, gw, gw), lambda g, b: (g, 0, 0)),
            pl.BlockSpec((1, gw * gps), lambda g, b: (0, g)),
        ],
        out_specs=pl.BlockSpec((seq_len, gw * gps), lambda g, b: (b, g)),
        out_shape=jax.ShapeDtypeStruct((m, groups * gw), BF16),
        compiler_params=_cparams(("arbitrary", "arbitrary"), 48),
        name="pool_mix",
    )(u, band, inv, w_pool, pool_scale.reshape(1, groups * gw))


_NT_DIMS = (((1,), (1,)), ((), ()))


def _ctx_attn_kernel(q_ref, k_ref, v_ref, o_ref):
    n_heads, _, dh = q_ref.shape
    for h in range(n_heads):
        k = k_ref[h].astype(BF16)
        v = v_ref[h].astype(BF16)
        s = lax.dot_general(q_ref[h], k, _NT_DIMS, preferred_element_type=F32)
        p = jnp.exp(s - jnp.max(s, axis=-1, keepdims=True))
        denom = jnp.sum(p, axis=-1, keepdims=True)
        o = jnp.dot(p.astype(BF16), v, preferred_element_type=F32) / denom
        o_ref[:, h * dh:(h + 1) * dh] = o.astype(o_ref.dtype)


def context_attention(q, k, v):
    b, n_heads, seq_len, dh = q.shape
    blk = pl.BlockSpec((None, n_heads, seq_len, dh), lambda i: (i, 0, 0, 0))
    return pl.pallas_call(
        _ctx_attn_kernel,
        grid=(b,),
        in_specs=[blk, blk, blk],
        out_specs=pl.BlockSpec((seq_len, n_heads * dh), lambda i: (i, 0)),
        out_shape=jax.ShapeDtypeStruct((b * seq_len, n_heads * dh), BF16),
        compiler_params=_cparams(("arbitrary",), 48),
        name="context_attention",
    )(q, k, v)


def _na_structure(rows):
    kr = min(NA_ROWS, rows)
    assert rows % NA_TILE_ROWS == 0 and rows >= NA_WIN_ROWS
    n_tiles = rows // NA_TILE_ROWS
    starts = np.clip(np.arange(n_tiles) * NA_TILE_ROWS - kr // 2, 0, rows - NA_WIN_ROWS)
    patterns, pattern_of = [], []
    for t in range(n_tiles):
        r = t * NA_TILE_ROWS + np.arange(NA_TILE_ROWS)[:, None]
        ka = starts[t] + np.arange(NA_WIN_ROWS)[None, :]
        rs = np.clip(r - kr // 2, 0, rows - kr)
        assert (rs >= starts[t]).all() and (rs + kr <= starts[t] + NA_WIN_ROWS).all()
        slot = np.where((ka >= rs) & (ka < rs + kr), ka - r + NA_ROWS - 1, 2 * NA_ROWS - 1)
        for p, known in enumerate(patterns):
            if (known == slot).all():
                pattern_of.append(p)
                break
        else:
            pattern_of.append(len(patterns))
            patterns.append(slot)
    return tuple(int(s) for s in starts), tuple(pattern_of), np.stack(patterns)


def _na_bias(rpb, dr_slot):
    n_heads = rpb.shape[0]
    qc = np.arange(GRID_W)[:, None]
    kc = np.arange(GRID_W)[None, :]
    col_start = np.clip(qc - NA_COLS // 2, 0, GRID_W - NA_COLS)
    col_ok = (kc >= col_start) & (kc < col_start + NA_COLS)
    dc = np.clip(kc - qc + NA_COLS - 1, 0, 2 * NA_COLS - 2)
    onehot = (dc.reshape(1, -1) == np.arange(2 * NA_COLS - 1)[:, None]).astype(np.float32)
    toep = jnp.einsum("hds,sn->hdn", rpb, jnp.asarray(onehot), precision=lax.Precision.HIGHEST)
    toep = jnp.where(jnp.asarray(col_ok)[None, None], toep.reshape(n_heads, 2 * NA_ROWS - 1, GRID_W, GRID_W), NEG)
    toep = jnp.concatenate([toep, jnp.full((n_heads, 1, GRID_W, GRID_W), NEG, F32)], axis=1)
    n_pat = dr_slot.shape[0]
    strips = [jnp.concatenate([toep[:, int(s)] for s in dr_slot[p, qr]], axis=-1)
              for p in range(n_pat) for qr in range(NA_TILE_ROWS)]
    return jnp.stack(strips, axis=1).reshape(n_heads, n_pat, NA_TILE_ROWS * GRID_W, NA_WIN_ROWS * GRID_W)


def _na_kernel(q_ref, k_ref, v_ref, ck_ref, cv_ref, bias_ref, o_ref, *, starts, pattern_of):
    tq = NA_TILE_ROWS * GRID_W
    tk = NA_WIN_ROWS * GRID_W
    ck = ck_ref[...].astype(BF16)
    cv = cv_ref[...].astype(BF16)

    def scores(t):
        q = q_ref[t * tq:(t + 1) * tq, :]
        kw = k_ref[starts[t] * GRID_W:starts[t] * GRID_W + tk, :]
        s_loc = lax.dot_general(q, kw, _NT_DIMS, preferred_element_type=F32) + bias_ref[pattern_of[t]]
        s_ctx = lax.dot_general(q, ck, _NT_DIMS, preferred_element_type=F32)
        return s_loc, s_ctx

    def finish(t, s_loc, s_ctx):
        vw = v_ref[starts[t] * GRID_W:starts[t] * GRID_W + tk, :]
        mx = jnp.maximum(jnp.max(s_loc, axis=-1, keepdims=True), jnp.max(s_ctx, axis=-1, keepdims=True))
        p_loc = jnp.exp(s_loc - mx)
        p_ctx = jnp.exp(s_ctx - mx)
        denom = jnp.sum(p_loc, axis=-1, keepdims=True) + jnp.sum(p_ctx, axis=-1, keepdims=True)
        o = (jnp.dot(p_loc.astype(BF16), vw, preferred_element_type=F32)
             + jnp.dot(p_ctx.astype(BF16), cv, preferred_element_type=F32)) / denom
        o_ref[t * tq:(t + 1) * tq, :] = o.astype(o_ref.dtype)

    n_tiles = len(starts)
    sc = scores(0)
    for t in range(n_tiles):
        nxt = scores(t + 1) if t + 1 < n_tiles else None
        finish(t, *sc)
        sc = nxt


def neighbourhood_attention(qkv, ctx_k, ctx_v, layer, rpb):
    b, n_heads, seq_len, dh = qkv.shape
    n_heads //= 3
    lc = ctx_k.shape[3]
    starts, pattern_of, dr_slot = _na_structure(seq_len // GRID_W)
    bias = _na_bias(rpb, dr_slot)
    _, n_pat, tq, tk = bias.shape
    part = lambda p: pl.BlockSpec((None, None, seq_len, dh), lambda h, i: (i, h + p * n_heads, 0, 0))
    ctx = pl.BlockSpec((None, None, None, lc, dh), lambda h, i: (i, layer, h, 0, 0))
    return pl.pallas_call(
        functools.partial(_na_kernel, starts=starts, pattern_of=pattern_of),
        grid=(n_heads, b),
        in_specs=[part(0), part(1), part(2), ctx, ctx,
                  pl.BlockSpec((None, n_pat, tq, tk), lambda h, i: (h, 0, 0, 0))],
        out_specs=pl.BlockSpec((seq_len, dh), lambda h, i: (i, h)),
        out_shape=jax.ShapeDtypeStruct((b * seq_len, n_heads * dh), BF16),
        compiler_params=_cparams(("arbitrary", "arbitrary"), 48),
        name="neighbourhood_attention",
    )(qkv, qkv, qkv, ctx_k, ctx_v, bias)


SHIFT1, SCALE1, GATE1, SHIFT2, SCALE2, GATE2 = range(6)


def _trunk(x, seq_len, mod, row_of, ctx_kv, wts, alpha):
    (ln1_g, ln1_b, ln2_g, ln2_b, w_in, w_four, w_pool, pool_scale, w_out_a, w_qkv, rpb, w_out_c,
     w_up, conv_w, conv_b, w_down) = wts
    depth = mod.shape[0]
    d = x.shape[1]
    new_k, new_v = [], []
    h = ln_modulate(x, mod[0], row_of, SCALE1, SHIFT1)
    for i in range(depth):
        j = i // 2
        if i % 2 == 0:
            u = matmul([h], w_in, j, n=d)
            ya = fourier_mix(u, w_four[j], seq_len)
            yb = pool_mix(u, w_pool[j], pool_scale[j], seq_len, w_four.shape[1])
            y = matmul([ya, yb], w_out_a, j, n=d)
        else:
            if ctx_kv is None:
                q = matmul([h], w_qkv, j, n=d, col_off=0, scale=HEAD_DIM ** -0.5, seq_len=seq_len)
                k = matmul([h], w_qkv, j, n=d, col_off=d, out_dtype=F32, seq_len=seq_len)
                v = matmul([h], w_qkv, j, n=d, col_off=2 * d, out_dtype=F32, seq_len=seq_len)
                new_k.append(k)
                new_v.append(v)
                o = context_attention(q, k, v)
            else:
                qkv = matmul([h], w_qkv, j, n=3 * d, scale=HEAD_DIM ** -0.5, scaled_cols=d, seq_len=seq_len,
                             tn=min(1024, d))
                o = neighbourhood_attention(qkv, ctx_kv[0], ctx_kv[1], j, rpb[j])
            y = matmul([o], w_out_c, j, n=d)
        x, h = residual_ln(x, y, mod[i], row_of, GATE1, ln1_g[i], ln1_b[i], alpha, nxt=(mod[i], SCALE2, SHIFT2))
        f = matmul_ksplit(ffn_up(h, w_up, conv_w, conv_b, i, seq_len), w_down, i)
        nxt = (mod[i + 1], SCALE1, SHIFT1) if i + 1 < depth else None
        x, h = residual_ln(x, f, mod[i], row_of, GATE2, ln2_g[i], ln2_b[i], alpha, nxt=nxt)
    return x, new_k, new_v


def kernel(x_prompt, x_sample, cache_k, cache_v, c, c_ctx, w_ada, b_ada, ln1_g, ln1_b, ln2_g, ln2_b,
           w_in, w_four, w_pool, pool_scale, w_out_a, w_qkv, rpb, w_out_c, w_up, conv_w, conv_b, w_down):
    batch, seq, d = x_prompt.shape
    dec_batch, dec_seq, _ = x_sample.shape
    depth = w_ada.shape[0]
    alpha = float((2 * depth) ** 0.25)
    assert dec_batch < COND_ROWS

    cond = jnp.concatenate([c, c_ctx[None, :], jnp.zeros((COND_ROWS - dec_batch - 1, d), F32)], axis=0)
    mod = ada_modulation(cond, w_ada, b_ada).reshape(depth, COND_ROWS, 1, 6 * d)

    wts = (ln1_g, ln1_b, ln2_g, ln2_b,
           w_in.astype(BF16), w_four.astype(BF16), w_pool.astype(BF16), pool_scale, w_out_a.astype(BF16),
           w_qkv.astype(BF16), rpb, w_out_c.astype(BF16),
           w_up.astype(BF16), conv_w, conv_b, w_down.astype(BF16))

    y_prompt, ks, vs = _trunk(x_prompt.reshape(batch * seq, d), seq, mod, lambda r: dec_batch, None, wts, alpha)
    y_sample, _, _ = _trunk(x_sample.reshape(dec_batch * dec_seq, d), dec_seq, mod, lambda r: r // dec_seq,
                            (cache_k, cache_v), wts, alpha)
    return (y_prompt.reshape(batch, seq, d), y_sample.reshape(dec_batch, dec_seq, d),
            jnp.stack(ks, axis=1), jnp.stack(vs, axis=1))
```

```python
import functools

import numpy as np
import jax
import jax.numpy as jnp
from jax import lax
from jax.experimental import pallas as pl
from jax.experimental.pallas import tpu as pltpu

F32 = jnp.float32
BF16 = jnp.bfloat16

LN_EPS = 1e-5
NEG = -1e30
GRID_W = 64
NA_ROWS = 8
NA_COLS = 16
HEAD_DIM = 128
POOL_WINDOWS = (2, 4, 8, 16)
COND_ROWS = 16
NA_TILE_ROWS = 4
NA_WIN_ROWS = 12
LN_CHUNK = 64
POOL_ROWS = 256
SHORT_SEQ = 512
MIB = 1024 * 1024


def _cparams(semantics, vmem_mib):
    return pltpu.CompilerParams(dimension_semantics=semantics, vmem_limit_bytes=int(vmem_mib * MIB))


def _tile(dim, pref):
    t = min(dim, pref)
    assert dim % t == 0, (dim, pref)
    return t


def _ada_kernel(c_ref, w_ref, b_ref, o_ref):
    c = c_ref[...]
    s = (c * jax.nn.sigmoid(c)).astype(BF16)
    o_ref[...] = jnp.dot(s, w_ref[...].astype(BF16), preferred_element_type=F32) + b_ref[...]


def ada_modulation(cond, w_ada, b_ada):
    depth, d, n = w_ada.shape
    tn = _tile(n, 512)
    return pl.pallas_call(
        _ada_kernel,
        grid=(depth, n // tn),
        in_specs=[
            pl.BlockSpec((COND_ROWS, d), lambda l, j: (0, 0)),
            pl.BlockSpec((None, d, tn), lambda l, j: (l, 0, j)),
            pl.BlockSpec((None, 1, tn), lambda l, j: (l, 0, j)),
        ],
        out_specs=pl.BlockSpec((None, COND_ROWS, tn), lambda l, j: (l, 0, j)),
        out_shape=jax.ShapeDtypeStruct((depth, COND_ROWS, n), F32),
        compiler_params=_cparams(("arbitrary", "arbitrary"), 40),
        name="ada_modulation",
    )(cond, w_ada, b_ada.reshape(depth, 1, n))


def _normalise(x):
    mu = jnp.mean(x, axis=-1, keepdims=True)
    xc = x - mu
    var = jnp.mean(xc * xc, axis=-1, keepdims=True)
    return xc * lax.rsqrt(var + LN_EPS)


def _row_chunks(n_rows, body):
    chunk = min(LN_CHUNK, n_rows)
    assert n_rows % chunk == 0

    def step(c, carry):
        body(pl.ds(pl.multiple_of(c * chunk, chunk), chunk))
        return carry

    lax.fori_loop(0, n_rows // chunk, step, 0)


def _ln_mod_kernel(x_ref, sc_ref, sh_ref, h_ref):
    scale = 1.0 + sc_ref[...]
    shift = sh_ref[...]

    def body(rows):
        h_ref[rows, :] = (_normalise(x_ref[rows, :]) * scale + shift).astype(h_ref.dtype)

    _row_chunks(x_ref.shape[0], body)


def _mod_spec(d, row_of_tile, which):
    return pl.BlockSpec((None, 1, d), lambda i: (row_of_tile(i), 0, which))


def ln_modulate(x, mod, row_of, which_scale, which_shift):
    m, d = x.shape
    tm = _tile(m, 512)
    row = lambda i: row_of(i * tm)
    return pl.pallas_call(
        _ln_mod_kernel,
        grid=(m // tm,),
        in_specs=[pl.BlockSpec((tm, d), lambda i: (i, 0)),
                  _mod_spec(d, row, which_scale), _mod_spec(d, row, which_shift)],
        out_specs=pl.BlockSpec((tm, d), lambda i: (i, 0)),
        out_shape=jax.ShapeDtypeStruct((m, d), BF16),
        compiler_params=_cparams(("arbitrary",), 40),
        name="ln_modulate",
    )(x, mod, mod)


def _res_ln_kernel(*refs, alpha, with_next):
    if with_next:
        x_ref, y_ref, g_ref, gam_ref, bet_ref, sc_ref, sh_ref, xo_ref, h_ref = refs
        scale = 1.0 + sc_ref[...]
        shift = sh_ref[...]
    else:
        x_ref, y_ref, g_ref, gam_ref, bet_ref, xo_ref = refs
    gate, gamma, beta = g_ref[...], gam_ref[...], bet_ref[...]

    def body(rows):
        z = alpha * x_ref[rows, :] + gate * y_ref[rows, :].astype(F32)
        xn = _normalise(z) * gamma + beta
        xo_ref[rows, :] = xn
        if with_next:
            h_ref[rows, :] = (_normalise(xn) * scale + shift).astype(h_ref.dtype)

    _row_chunks(x_ref.shape[0], body)


def residual_ln(x, y, mod, row_of, which_gate, gamma, beta, alpha, nxt=None):
    m, d = x.shape
    tm = _tile(m, 256)
    row = lambda i: row_of(i * tm)
    tile = pl.BlockSpec((tm, d), lambda i: (i, 0))
    tile_in = pl.BlockSpec((tm, d), lambda i: (i, 0), pipeline_mode=pl.Buffered(3))
    vec = pl.BlockSpec((1, d), lambda i: (0, 0))
    in_specs = [tile_in, tile_in, _mod_spec(d, row, which_gate), vec, vec]
    args = [x, y, mod, gamma.reshape(1, d), beta.reshape(1, d)]
    out_specs = [tile]
    out_shape = [jax.ShapeDtypeStruct((m, d), F32)]
    if nxt is not None:
        in_specs += [_mod_spec(d, row, nxt[1]), _mod_spec(d, row, nxt[2])]
        args += [nxt[0], nxt[0]]
        out_specs.append(tile)
        out_shape.append(jax.ShapeDtypeStruct((m, d), BF16))
    body = functools.partial(_res_ln_kernel, alpha=alpha, with_next=nxt is not None)

    def streamed(*refs):
        pltpu.emit_pipeline(body, grid=(m // tm,), in_specs=in_specs, out_specs=out_specs)(*refs)

    anywhere = pl.BlockSpec(memory_space=pl.ANY)
    out = pl.pallas_call(
        streamed,
        in_specs=[anywhere] * len(args),
        out_specs=[anywhere] * len(out_shape),
        out_shape=out_shape,
        compiler_params=pltpu.CompilerParams(vmem_limit_bytes=56 * MIB),
        name="residual_ln",
    )(*args)
    return (out[0], out[1]) if nxt is not None else (out[0], None)


def _mm_kernel(*refs, n_parts, scale, scaled_tiles, head_major):
    a_refs, w_refs, o_ref = refs[:n_parts], refs[n_parts:2 * n_parts], refs[2 * n_parts]
    acc = None
    for a_ref, w_ref in zip(a_refs, w_refs):
        part = jnp.dot(a_ref[...], w_ref[...], preferred_element_type=F32)
        acc = part if acc is None else acc + part
    if scale is not None:
        acc = acc * jnp.where(pl.program_id(1) < scaled_tiles, scale, 1.0)
    if not head_major:
        o_ref[...] = acc.astype(o_ref.dtype)
        return
    bt, nh, lb, dh = o_ref.shape
    for b in range(bt):
        for h in range(nh):
            o_ref[b, h] = acc[b * lb:(b + 1) * lb, h * dh:(h + 1) * dh].astype(o_ref.dtype)


def matmul(a_parts, w, layer, *, n, col_off=0, out_dtype=BF16, scale=None, scaled_cols=None, seq_len=None,
           tm=1024, tn=1024):
    m, kp = a_parts[0].shape
    assert all(a.shape == (m, kp) for a in a_parts) and w.shape[1] == kp * len(a_parts)
    tm, tn = _tile(m, tm), _tile(n, tn)
    scaled_cols = n if scaled_cols is None else scaled_cols
    assert col_off % tn == 0 and scaled_cols % tn == 0
    joff = col_off // tn
    in_specs = [pl.BlockSpec((tm, kp), lambda i, j: (i, 0)) for _ in a_parts]
    in_specs += [pl.BlockSpec((None, kp, tn), functools.partial(lambda i, j, p: (layer, p, j + joff), p=p))
                 for p in range(len(a_parts))]
    if seq_len is None:
        out_spec = pl.BlockSpec((tm, tn), lambda i, j: (i, j))
        out_shape = jax.ShapeDtypeStruct((m, n), out_dtype)
    else:
        nh = tn // HEAD_DIM
        out_shape = jax.ShapeDtypeStruct((m // seq_len, n // HEAD_DIM, seq_len, HEAD_DIM), out_dtype)
        if tm >= seq_len:
            out_spec = pl.BlockSpec((tm // seq_len, nh, seq_len, HEAD_DIM), lambda i, j: (i, j, 0, 0))
        else:
            per = seq_len // tm
            out_spec = pl.BlockSpec((1, nh, tm, HEAD_DIM), lambda i, j: (i // per, j, i % per, 0))
    return pl.pallas_call(
        functools.partial(_mm_kernel, n_parts=len(a_parts), scale=scale, scaled_tiles=scaled_cols // tn,
                          head_major=seq_len is not None),
        grid=(m // tm, n // tn),
        in_specs=in_specs,
        out_specs=out_spec,
        out_shape=out_shape,
        compiler_params=_cparams(("arbitrary", "arbitrary"), 56),
        name="matmul",
    )(*a_parts, *([w] * len(a_parts)))


def _mm_ksplit_kernel(a_ref, w_ref, o_ref, acc_ref, *, k_steps):
    part = jnp.dot(a_ref[...], w_ref[...], preferred_element_type=F32)
    s = pl.program_id(2)

    @pl.when(s == 0)
    def _():
        acc_ref[...] = part

    @pl.when((s > 0) & (s < k_steps - 1))
    def _():
        acc_ref[...] += part

    @pl.when(s == k_steps - 1)
    def _():
        o_ref[...] = (acc_ref[...] + part).astype(o_ref.dtype)


def matmul_ksplit(a, w, layer, *, out_dtype=BF16, tm=1024, tn=512, k_steps=2):
    m, k = a.shape
    n = w.shape[2]
    tm, tn = _tile(m, tm), _tile(n, tn)
    tk = k // k_steps
    assert k_steps >= 2 and tk * k_steps == k and tk % 128 == 0
    return pl.pallas_call(
        functools.partial(_mm_ksplit_kernel, k_steps=k_steps),
        grid=(m // tm, n // tn, k_steps),
        in_specs=[pl.BlockSpec((tm, tk), lambda i, j, s: (i, s)),
                  pl.BlockSpec((None, tk, tn), lambda i, j, s: (layer, s, j))],
        out_specs=pl.BlockSpec((tm, tn), lambda i, j, s: (i, j)),
        out_shape=jax.ShapeDtypeStruct((m, n), out_dtype),
        scratch_shapes=[pltpu.VMEM((tm, tn), F32)],
        compiler_params=_cparams(("arbitrary", "arbitrary", "arbitrary"), 56),
        name="matmul_ksplit",
    )(a, w)


def _ffn_up_kernel(a_ref, halo_ref, wa_ref, wg_ref, cwa_ref, cwg_ref, o_ref, *, seq_len):
    a = a_ref[...]
    halo = halo_ref[...]
    tm, tn = o_ref.shape
    sub = lax.broadcasted_iota(jnp.int32, (8, tn), 0)
    inner = range(seq_len, tm, seq_len)

    def patch(x, group, row, value):
        lo, hi = group * 8, (group + 1) * 8
        parts = [x[:lo]] * (lo > 0) + [jnp.where(sub == row, value, x[lo:hi])] + [x[hi:]] * (hi < tm)
        return jnp.concatenate(parts, axis=0)

    def conv(w_ref, cw_ref):
        w = w_ref[...]
        u = jnp.dot(a, w, preferred_element_type=F32)
        uh = jnp.dot(halo, w, preferred_element_type=F32)
        prev = patch(pltpu.roll(u, 1, 0), 0, 0, uh[7:8])
        nxt = patch(pltpu.roll(u, tm - 1, 0), tm // 8 - 1, 7, uh[0:1])
        for r in inner:
            prev = patch(prev, r // 8, 0, 0.0)
            nxt = patch(nxt, r // 8 - 1, 7, 0.0)
        cw = cw_ref[...]
        return prev * cw[0:1] + u * cw[1:2] + nxt * cw[2:3] + cw[3:4]

    act = conv(wa_ref, cwa_ref)
    half = 0.5 * conv(wg_ref, cwg_ref)
    o_ref[...] = ((half + half * jnp.tanh(half)) * act).astype(o_ref.dtype)


def ffn_up(h, w_up, conv_w, conv_b, layer, seq_len, *, tm=2048, tn=256):
    m, d = h.shape
    d_ff = w_up.shape[2] // 2
    tm, tn = _tile(m, tm), _tile(d_ff, tn)
    assert (tm % seq_len == 0 or seq_len % tm == 0) and tm % 16 == 0
    nt, nj = m // tm, d_ff // tn
    tiles = h.reshape(nt, tm, d)
    zero = jnp.zeros((1, d), h.dtype)
    starts = np.arange(nt) * tm
    before = jnp.concatenate([zero, tiles[:-1, -1]], axis=0)
    before = jnp.where(jnp.asarray(starts % seq_len == 0)[:, None], 0, before)
    after = jnp.concatenate([tiles[1:, 0], zero], axis=0)
    after = jnp.where(jnp.asarray((starts + tm) % seq_len == 0)[:, None], 0, after)
    halo = jnp.concatenate([after[:, None], jnp.zeros((nt, 6, d), h.dtype), before[:, None]], axis=1)
    taps = jnp.concatenate([conv_w[layer], conv_b[layer][None]], axis=0)
    return pl.pallas_call(
        functools.partial(_ffn_up_kernel, seq_len=seq_len),
        grid=(nt, nj),
        in_specs=[
            pl.BlockSpec((tm, d), lambda i, j: (i, 0)),
            pl.BlockSpec((8, d), lambda i, j: (i, 0)),
            pl.BlockSpec((None, d, tn), lambda i, j: (layer, 0, j)),
            pl.BlockSpec((None, d, tn), lambda i, j: (layer, 0, j + nj)),
            pl.BlockSpec((4, tn), lambda i, j: (0, j)),
            pl.BlockSpec((4, tn), lambda i, j: (0, j + nj)),
        ],
        out_specs=pl.BlockSpec((tm, tn), lambda i, j: (i, j)),
        out_shape=jax.ShapeDtypeStruct((m, d_ff), BF16),
        compiler_params=_cparams(("arbitrary", "arbitrary"), 56),
        name="ffn_up",
    )(h, halo.reshape(nt * 8, d), w_up, w_up, taps, taps)


def _dft_matrices(n):
    k = jnp.arange(n, dtype=jnp.int32)
    phase = ((k[:, None] * k[None, :]) % n).astype(F32) * (2.0 * np.pi / n)
    scale = 1.0 / np.sqrt(n)
    return (jnp.cos(phase) * scale).astype(BF16), (jnp.sin(phase) * scale).astype(BF16)


def _fourier_kernel(u_ref, ch_ref, sh_ref, cmid_ref, flip_ref, cc_ref, sc_ref, w_ref, o_ref):
    n = u_ref.shape[0]
    hn = n // 2
    gw = cc_ref.shape[0]
    sub = lax.broadcasted_iota(jnp.int32, (8, gw), 0)
    for g in range(w_ref.shape[0]):
        cols = slice(g * gw, (g + 1) * gw)
        u = u_ref[:, cols]
        zc = jnp.dot(u, cc_ref[...], preferred_element_type=F32).astype(BF16)
        zs = jnp.dot(u, sc_ref[...], preferred_element_type=F32).astype(BF16)
        p1 = jnp.dot(ch_ref[...], zc, preferred_element_type=F32)
        p2 = jnp.dot(sh_ref[...], zs, preferred_element_type=F32)
        mid = jnp.dot(cmid_ref[...], zc, preferred_element_type=F32)
        w = w_ref[g]
        o_ref[0:hn, cols] = jnp.dot((p1 - p2).astype(BF16), w, preferred_element_type=F32).astype(o_ref.dtype)
        back = jnp.dot(flip_ref[...], (p1 + p2).astype(BF16), preferred_element_type=F32)
        back = jnp.concatenate([jnp.where(sub == 0, mid[0:1], back[0:8]), back[8:]], axis=0)
        o_ref[hn:n, cols] = jnp.dot(back.astype(BF16), w, preferred_element_type=F32).astype(o_ref.dtype)


def fourier_mix(u, w_four, seq_len):
    m = u.shape[0]
    groups, gw, _ = w_four.shape
    gps = groups if seq_len <= SHORT_SEQ else 1
    hn = seq_len // 2
    assert hn % 8 == 0
    cl, sl = _dft_matrices(seq_len)
    cc, sc = _dft_matrices(gw)
    r = jnp.arange(hn, dtype=jnp.int32)
    flip = ((r[:, None] + r[None, :] == hn) & (r[:, None] >= 1)).astype(BF16)
    const = lambda shape, idx: pl.BlockSpec(shape, lambda b, g: idx, pipeline_mode=pl.Buffered(1))
    return pl.pallas_call(
        _fourier_kernel,
        grid=(m // seq_len, groups // gps),
        in_specs=[
            pl.BlockSpec((seq_len, gw * gps), lambda b, g: (b, g)),
            const((hn, seq_len), (0, 0)), const((hn, seq_len), (0, 0)), const((8, seq_len), (hn // 8, 0)),
            const((hn, hn), (0, 0)), const((gw, gw), (0, 0)), const((gw, gw), (0, 0)),
            pl.BlockSpec((gps, gw, gw), lambda b, g: (g, 0, 0)),
        ],
        out_specs=pl.BlockSpec((seq_len, gw * gps), lambda b, g: (b, g)),
        out_shape=jax.ShapeDtypeStruct((m, groups * gw), BF16),
        compiler_params=_cparams(("arbitrary", "arbitrary"), 56),
        name="fourier_mix",
    )(u, cl, sl, cl, flip, cc, sc, w_four)


def _pool_matrices(seq_len):
    t = jnp.arange(seq_len, dtype=jnp.int32)
    bands, inv = [], []
    for w in POOL_WINDOWS:
        lo = jnp.clip(t - w // 2, 0, seq_len)
        hi = jnp.clip(t - w // 2 + w, 0, seq_len)
        bands.append(((t[None, :] >= lo[:, None]) & (t[None, :] < hi[:, None])).astype(BF16))
        inv.append(1.0 / (hi - lo).astype(F32))
    return jnp.stack(bands), jnp.stack(inv)[:, :, None]


def _pool_kernel(u_ref, band_ref, inv_ref, w_ref, ps_ref, o_ref):
    seq_len = u_ref.shape[0]
    rows = min(POOL_ROWS, seq_len)
    span = min(2 * rows, seq_len)
    assert seq_len % rows == 0 and max(POOL_WINDOWS) <= rows // 2
    gw = w_ref.shape[-1]
    for g in range(w_ref.shape[0]):
        cols = slice(g * gw, (g + 1) * gw)
        for r0 in range(0, seq_len, rows):
            c0 = min(max(r0 - rows // 2, 0), seq_len - span)
            u = u_ref[c0:c0 + span, cols]
            sums = jnp.dot(band_ref[g, r0:r0 + rows, c0:c0 + span], u, preferred_element_type=F32)
            pooled = sums * inv_ref[g, r0:r0 + rows, :] - u_ref[r0:r0 + rows, cols].astype(F32)
            y = jnp.dot(pooled.astype(BF16), w_ref[g], preferred_element_type=F32)
            o_ref[r0:r0 + rows, cols] = (y * ps_ref[:, cols]).astype(o_ref.dtype)


def pool_mix(u, w_pool, pool_scale, seq_len, group_off):
    m = u.shape[0]
    groups, gw, _ = w_pool.shape
    gps = groups if seq_len <= SHORT_SEQ else 1
    assert group_off % gps == 0
    band, inv = _pool_matrices(seq_len)
    return pl.pallas_call(
        _pool_kernel,
        grid=(groups // gps, m // seq_len),
        in_specs=[
            pl.BlockSpec((seq_len, gw * gps), lambda g, b: (b, g + group_off // gps)),
            pl.BlockSpec((gps, seq_len, seq_len), lambda g, b: (g, 0, 0)),
            pl.BlockSpec((gps, seq_len, 1), lambda g, b: (g, 0, 0)),
            pl.BlockSpec((gps, gw, gw), lambda g, b: (g, 0, 0)),
            pl.BlockSpec((1, gw * gps), lambda g, b: (0, g)),
        ],
        out_specs=pl.BlockSpec((seq_len, gw * gps), lambda g, b: (b, g)),
        out_shape=jax.ShapeDtypeStruct((m, groups * gw), BF16),
        compiler_params=_cparams(("arbitrary", "arbitrary"), 48),
        name="pool_mix",
    )(u, band, inv, w_pool, pool_scale.reshape(1, groups * gw))


_NT_DIMS = (((1,), (1,)), ((), ()))


def _ctx_attn_kernel(q_ref, k_ref, v_ref, o_ref):
    n_heads, _, dh = q_ref.shape
    for h in range(n_heads):
        k = k_ref[h].astype(BF16)
        v = v_ref[h].astype(BF16)
        s = lax.dot_general(q_ref[h], k, _NT_DIMS, preferred_element_type=F32)
        p = jnp.exp(s - jnp.max(s, axis=-1, keepdims=True))
        denom = jnp.sum(p, axis=-1, keepdims=True)
        o = jnp.dot(p.astype(BF16), v, preferred_element_type=F32) / denom
        o_ref[:, h * dh:(h + 1) * dh] = o.astype(o_ref.dtype)


def context_attention(q, k, v):
    b, n_heads, seq_len, dh = q.shape
    blk = pl.BlockSpec((None, n_heads, seq_len, dh), lambda i: (i, 0, 0, 0))
    return pl.pallas_call(
        _ctx_attn_kernel,
        grid=(b,),
        in_specs=[blk, blk, blk],
        out_specs=pl.BlockSpec((seq_len, n_heads * dh), lambda i: (i, 0)),
        out_shape=jax.ShapeDtypeStruct((b * seq_len, n_heads * dh), BF16),
        compiler_params=_cparams(("arbitrary",), 48),
        name="context_attention",
    )(q, k, v)


def _na_structure(rows):
    kr = min(NA_ROWS, rows)
    assert rows % NA_TILE_ROWS == 0 and rows >= NA_WIN_ROWS
    n_tiles = rows // NA_TILE_ROWS
    starts = np.clip(np.arange(n_tiles) * NA_TILE_ROWS - kr // 2, 0, rows - NA_WIN_ROWS)
    patterns, pattern_of = [], []
    for t in range(n_tiles):
        r = t * NA_TILE_ROWS + np.arange(NA_TILE_ROWS)[:, None]
        ka = starts[t] + np.arange(NA_WIN_ROWS)[None, :]
        rs = np.clip(r - kr // 2, 0, rows - kr)
        assert (rs >= starts[t]).all() and (rs + kr <= starts[t] + NA_WIN_ROWS).all()
        slot = np.where((ka >= rs) & (ka < rs + kr), ka - r + NA_ROWS - 1, 2 * NA_ROWS - 1)
        for p, known in enumerate(patterns):
            if (known == slot).all():
                pattern_of.append(p)
                break
        else:
            pattern_of.append(len(patterns))
            patterns.append(slot)
    return tuple(int(s) for s in starts), tuple(pattern_of), np.stack(patterns)


def _na_bias(rpb, dr_slot):
    n_heads = rpb.shape[0]
    qc = np.arange(GRID_W)[:, None]
    kc = np.arange(GRID_W)[None, :]
    col_start = np.clip(qc - NA_COLS // 2, 0, GRID_W - NA_COLS)
    col_ok = (kc >= col_start) & (kc < col_start + NA_COLS)
    dc = np.clip(kc - qc + NA_COLS - 1, 0, 2 * NA_COLS - 2)
    onehot = (dc.reshape(1, -1) == np.arange(2 * NA_COLS - 1)[:, None]).astype(np.float32)
    toep = jnp.einsum("hds,sn->hdn", rpb, jnp.asarray(onehot), precision=lax.Precision.HIGHEST)
    toep = jnp.where(jnp.asarray(col_ok)[None, None], toep.reshape(n_heads, 2 * NA_ROWS - 1, GRID_W, GRID_W), NEG)
    toep = jnp.concatenate([toep, jnp.full((n_heads, 1, GRID_W, GRID_W), NEG, F32)], axis=1)
    n_pat = dr_slot.shape[0]
    strips = [jnp.concatenate([toep[:, int(s)] for s in dr_slot[p, qr]], axis=-1)
              for p in range(n_pat) for qr in range(NA_TILE_ROWS)]
    return jnp.stack(strips, axis=1).reshape(n_heads, n_pat, NA_TILE_ROWS * GRID_W, NA_WIN_ROWS * GRID_W)


def _na_kernel(q_ref, k_ref, v_ref, ck_ref, cv_ref, bias_ref, o_ref, *, starts, pattern_of):
    tq = NA_TILE_ROWS * GRID_W
    tk = NA_WIN_ROWS * GRID_W
    ck = ck_ref[...].astype(BF16)
    cv = cv_ref[...].astype(BF16)

    def scores(t):
        q = q_ref[t * tq:(t + 1) * tq, :]
        kw = k_ref[starts[t] * GRID_W:starts[t] * GRID_W + tk, :]
        s_loc = lax.dot_general(q, kw, _NT_DIMS, preferred_element_type=F32) + bias_ref[pattern_of[t]]
        s_ctx = lax.dot_general(q, ck, _NT_DIMS, preferred_element_type=F32)
        return s_loc, s_ctx

    def finish(t, s_loc, s_ctx):
        vw = v_ref[starts[t] * GRID_W:starts[t] * GRID_W + tk, :]
        mx = jnp.maximum(jnp.max(s_loc, axis=-1, keepdims=True), jnp.max(s_ctx, axis=-1, keepdims=True))
        p_loc = jnp.exp(s_loc - mx)
        p_ctx = jnp.exp(s_ctx - mx)
        denom = jnp.sum(p_loc, axis=-1, keepdims=True) + jnp.sum(p_ctx, axis=-1, keepdims=True)
        o = (jnp.dot(p_loc.astype(BF16), vw, preferred_element_type=F32)
             + jnp.dot(p_ctx.astype(BF16), cv, preferred_element_type=F32)) / denom
        o_ref[t * tq:(t + 1) * tq, :] = o.astype(o_ref.dtype)

    n_tiles = len(starts)
    sc = scores(0)
    for t in range(n_tiles):
        nxt = scores(t + 1) if t + 1 < n_tiles else None
        finish(t, *sc)
        sc = nxt


def neighbourhood_attention(qkv, ctx_k, ctx_v, layer, rpb):
    b, n_heads, seq_len, dh = qkv.shape
    n_heads //= 3
    lc = ctx_k.shape[3]
    starts, pattern_of, dr_slot = _na_structure(seq_len // GRID_W)
    bias = _na_bias(rpb, dr_slot)
    _, n_pat, tq, tk = bias.shape
    part = lambda p: pl.BlockSpec((None, None, seq_len, dh), lambda h, i: (i, h + p * n_heads, 0, 0))
    ctx = pl.BlockSpec((None, None, None, lc, dh), lambda h, i: (i, layer, h, 0, 0))
    return pl.pallas_call(
        functools.partial(_na_kernel, starts=starts, pattern_of=pattern_of),
        grid=(n_heads, b),
        in_specs=[part(0), part(1), part(2), ctx, ctx,
                  pl.BlockSpec((None, n_pat, tq, tk), lambda h, i: (h, 0, 0, 0))],
        out_specs=pl.BlockSpec((seq_len, dh), lambda h, i: (i, h)),
        out_shape=jax.ShapeDtypeStruct((b * seq_len, n_heads * dh), BF16),
        compiler_params=_cparams(("arbitrary", "arbitrary"), 48),
        name="neighbourhood_attention",
    )(qkv, qkv, qkv, ctx_k, ctx_v, bias)


SHIFT1, SCALE1, GATE1, SHIFT2, SCALE2, GATE2 = range(6)


def _trunk(x, seq_len, mod, row_of, ctx_kv, wts, alpha):
    (ln1_g, ln1_b, ln2_g, ln2_b, w_in, w_four, w_pool, pool_scale, w_out_a, w_qkv, rpb, w_out_c,
     w_up, conv_w, conv_b, w_down) = wts
    depth = mod.shape[0]
    d = x.shape[1]
    new_k, new_v = [], []
    h = ln_modulate(x, mod[0], row_of, SCALE1, SHIFT1)
    for i in range(depth):
        j = i // 2
        if i % 2 == 0:
            u = matmul([h], w_in, j, n=d)
            ya = fourier_mix(u, w_four[j], seq_len)
            yb = pool_mix(u, w_pool[j], pool_scale[j], seq_len, w_four.shape[1])
            y = matmul([ya, yb], w_out_a, j, n=d)
        else:
            if ctx_kv is None:
                q = matmul([h], w_qkv, j, n=d, col_off=0, scale=HEAD_DIM ** -0.5, seq_len=seq_len)
                k = matmul([h], w_qkv, j, n=d, col_off=d, out_dtype=F32, seq_len=seq_len)
                v = matmul([h], w_qkv, j, n=d, col_off=2 * d, out_dtype=F32, seq_len=seq_len)
                new_k.append(k)
                new_v.append(v)
                o = context_attention(q, k, v)
            else:
                qkv = matmul([h], w_qkv, j, n=3 * d, scale=HEAD_DIM ** -0.5, scaled_cols=d, seq_len=seq_len,
                             tn=min(1024, d))
                o = neighbourhood_attention(qkv, ctx_kv[0], ctx_kv[1], j, rpb[j])
            y = matmul([o], w_out_c, j, n=d)
        x, h = residual_ln(x, y, mod[i], row_of, GATE1, ln1_g[i], ln1_b[i], alpha, nxt=(mod[i], SCALE2, SHIFT2))
        f = matmul_ksplit(ffn_up(h, w_up, conv_w, conv_b, i, seq_len), w_down, i)
        nxt = (mod[i + 1], SCALE1, SHIFT1) if i + 1 < depth else None
        x, h = residual_ln(x, f, mod[i], row_of, GATE2, ln2_g[i], ln2_b[i], alpha, nxt=nxt)
    return x, new_k, new_v


def kernel(x_prompt, x_sample, cache_k, cache_v, c, c_ctx, w_ada, b_ada, ln1_g, ln1_b, ln2_g, ln2_b,
           w_in, w_four, w_pool, pool_scale, w_out_a, w_qkv, rpb, w_out_c, w_up, conv_w, conv_b, w_down):
    batch, seq, d = x_prompt.shape
    dec_batch, dec_seq, _ = x_sample.shape
    depth = w_ada.shape[0]
    alpha = float((2 * depth) ** 0.25)
    assert dec_batch < COND_ROWS

    cond = jnp.concatenate([c, c_ctx[None, :], jnp.zeros((COND_ROWS - dec_batch - 1, d), F32)], axis=0)
    mod = ada_modulation(cond, w_ada, b_ada).reshape(depth, COND_ROWS, 1, 6 * d)

    wts = (ln1_g, ln1_b, ln2_g, ln2_b,
           w_in.astype(BF16), w_four.astype(BF16), w_pool.astype(BF16), pool_scale, w_out_a.astype(BF16),
           w_qkv.astype(BF16), rpb, w_out_c.astype(BF16),
           w_up.astype(BF16), conv_w, conv_b, w_down.astype(BF16))

    y_prompt, ks, vs = _trunk(x_prompt.reshape(batch * seq, d), seq, mod, lambda r: dec_batch, None, wts, alpha)
    y_sample, _, _ = _trunk(x_sample.reshape(dec_batch * dec_seq, d), dec_seq, mod, lambda r: r // dec_seq,
                            (cache_k, cache_v), wts, alpha)
    return (y_prompt.reshape(batch, seq, d), y_sample.reshape(dec_batch, dec_seq, d),
            jnp.stack(ks, axis=1), jnp.stack(vs, axis=1))
```
